```python
import math
import jax, jax.numpy as jnp
from jax import lax
import numpy as np

D_MODEL = 1024
BATCH = 8
SEQ = 2048
DEPTH = 2
DEC_BATCH = 8
DEC_SEQ = 8192
PAST_LEN = 128

HEAD_DIM = 64
D_MIX = D_MODEL
D_DIL = 3 * D_MIX // 8
D_NA = D_MIX // 4
D_CONV = D_MIX - D_DIL - D_NA
N_DIL_HEADS = D_DIL // HEAD_DIM
N_NA_HEADS = D_NA // HEAD_DIM
D_IN = 3 * D_MIX
CONV_WIDTH = 3
DIL_BRANCHES = ((128, 1), (512, 4), (2048, 16))
DIL_BLOCK = 128
GRID_W = 64
NA_ROWS = 8
NA_COLS = 16
NA_COL_BLOCK = 16
NA_KEY_COLS = 32
ROPE_THETA = 10000.0
D_FF = -(-8 * D_MODEL // (3 * 256)) * 256
RMS_EPS = 1e-6

kernel_name = 'hybrid_bidir_conv_dilated_natten_encoder'


def rmsnorm(x, g):
    xf = x.astype(jnp.float32)
    y = xf * lax.rsqrt(jnp.mean(xf * xf, axis=-1, keepdims=True) + RMS_EPS)
    return (y * g.astype(jnp.float32)).astype(x.dtype)


def rope(x, pos):
    hd = x.shape[-1]
    inv = 1.0 / (ROPE_THETA ** (jnp.arange(0, hd, 2, dtype=jnp.float32) / hd))
    ang = pos[:, None] * inv[None, :]
    cos = jnp.cos(ang)[None, :, None, :]
    sin = jnp.sin(ang)[None, :, None, :]
    x1, x2 = jnp.split(x.astype(jnp.float32), 2, axis=-1)
    out = jnp.concatenate([x1 * cos - x2 * sin, x1 * sin + x2 * cos], axis=-1)
    return out.astype(x.dtype)


def masked_softmax(scores, mask):
    s = jnp.where(mask, scores, -jnp.inf)
    lse = jax.nn.logsumexp(s, axis=-1, keepdims=True)
    return jnp.exp(s - lse), lse[..., 0]


def dilated_branch(q, k, v, window, dilation):
    b, h, s, hd = q.shape
    half = window // (2 * dilation)
    L = s // dilation
    qb = math.gcd(L, DIL_BLOCK)
    nb = L // qb
    klen = qb + 2 * half

    def residues(t):
        return t.reshape(b, h, L, dilation, hd).swapaxes(2, 3)

    qr = residues(q).reshape(b, h, dilation, nb, qb, hd)
    pad = ((0, 0), (0, 0), (0, 0), (half, half), (0, 0))
    kp = jnp.pad(residues(k), pad)
    vp = jnp.pad(residues(v), pad)
    blk = np.arange(nb)
    kk = np.arange(klen)
    qq = np.arange(qb)
    idx = blk[:, None] * qb + kk[None, :]
    kw = kp[:, :, :, idx, :]
    vw = vp[:, :, :, idx, :]
    scores = jnp.einsum('bhrnqe,bhrnke->bhrnqk', qr, kw,
                        preferred_element_type=jnp.float32) * (hd ** -0.5)
    off = kk[None, :] - half - qq[:, None]
    key_pos = idx - half
    mask = (np.abs(off) <= half)[None] & ((key_pos >= 0) & (key_pos < L))[:, None, :]
    p, lse = masked_softmax(scores, mask)
    out = jnp.einsum('bhrnqk,bhrnke->bhrnqe', p.astype(v.dtype), vw,
                     preferred_element_type=jnp.float32)
    out = out.reshape(b, h, dilation, L, hd).swapaxes(2, 3).reshape(b, h, s, hd)
    lse = lse.reshape(b, h, dilation, L).swapaxes(2, 3).reshape(b, h, s)
    return out, lse


def dilated_attention(q, k, v):
    outs, lses = [], []
    for window, dilation in DIL_BRANCHES:
        o, l = dilated_branch(q, k, v, window, dilation)
        outs.append(o)
        lses.append(l)
    w = jax.nn.softmax(jnp.stack(lses, axis=0), axis=0)
    return jnp.einsum('gbhs,gbhse->bhse', w, jnp.stack(outs, axis=0))


def neighbourhood_attention(q, k, v, rpb):
    b, h, s, hd = q.shape
    rows = s // GRID_W
    wr = min(NA_ROWS, rows)
    ncb = GRID_W // NA_COL_BLOCK
    nk = wr * NA_KEY_COLS
    r = np.arange(rows)
    row_start = np.clip(r - wr // 2, 0, rows - wr)
    key_rows = row_start[:, None] + np.arange(wr)
    c0 = np.arange(ncb) * NA_COL_BLOCK
    kc0 = np.clip(c0 - NA_COLS // 2, 0, GRID_W - NA_KEY_COLS)
    key_cols = kc0[:, None] + np.arange(NA_KEY_COLS)
    key_idx = (key_rows[:, None, :, None] * GRID_W
               + key_cols[None, :, None, :]).reshape(rows, ncb, nk)
    kg = k[:, :, key_idx, :]
    vg = v[:, :, key_idx, :]
    qg = q.reshape(b, h, rows, ncb, NA_COL_BLOCK, hd)
    scores = jnp.einsum('bhrcqe,bhrcke->bhrcqk', qg, kg,
                        preferred_element_type=jnp.float32) * (hd ** -0.5)
    qc = c0[:, None] + np.arange(NA_COL_BLOCK)
    col_start = np.clip(qc - NA_COLS // 2, 0, GRID_W - NA_COLS)
    kcol = np.broadcast_to(key_cols[:, None, :], (ncb, wr, NA_KEY_COLS)).reshape(ncb, nk)
    col_ok = (kcol[:, None, :] >= col_start[:, :, None]) & (kcol[:, None, :] < col_start[:, :, None] + NA_COLS)
    drow = np.broadcast_to(key_rows[:, :, None] - r[:, None, None], (rows, wr, NA_KEY_COLS)).reshape(rows, nk)
    dcol = kcol[:, None, :] - qc[:, :, None]
    ri = drow + NA_ROWS - 1
    ci = np.clip(dcol + NA_COLS - 1, 0, 2 * NA_COLS - 2)
    bias = rpb[:, ri[:, None, None, :], ci[None, :, :, :]]
    scores = scores + bias[None].astype(jnp.float32)
    p, _ = masked_softmax(scores, col_ok)
    out = jnp.einsum('bhrcqk,bhrcke->bhrcqe', p.astype(v.dtype), vg,
                     preferred_element_type=jnp.float32)
    return out.reshape(b, h, s, hd)


def short_conv_mixer(gate_b, gate_c, h_in, conv_w):
    u = gate_c * h_in
    conv = lax.conv_general_dilated(
        u, conv_w[:, None, :].astype(u.dtype), window_strides=(1,),
        padding=((CONV_WIDTH // 2, CONV_WIDTH // 2),),
        dimension_numbers=('NWC', 'WIO', 'NWC'), feature_group_count=u.shape[-1])
    return gate_b * conv


def encoder_layer(x, g_pre_mix, w_in, conv_w, rpb, w_out, g_post_mix,
                  g_pre_ffn, w_gate, w_up, w_down, g_post_ffn):
    b, s, _ = x.shape
    hn = rmsnorm(x, g_pre_mix)
    proj = hn @ w_in
    sizes = [D_DIL] * 3 + [D_NA] * 3 + [D_CONV] * 3
    cuts = list(np.cumsum(sizes)[:-1])
    q_d, k_d, v_d, q_n, k_n, v_n, gate_b, gate_c, h_in = jnp.split(proj, cuts, axis=-1)

    pos = jnp.arange(s, dtype=jnp.float32)
    heads = lambda t, n: t.reshape(b, s, n, HEAD_DIM)
    to_bhsd = lambda t: t.transpose(0, 2, 1, 3)
    qd = to_bhsd(rope(heads(q_d, N_DIL_HEADS), pos))
    kd = to_bhsd(rope(heads(k_d, N_DIL_HEADS), pos))
    vd = to_bhsd(heads(v_d, N_DIL_HEADS))
    y_dil = dilated_attention(qd, kd, vd).astype(x.dtype)
    y_dil = y_dil.transpose(0, 2, 1, 3).reshape(b, s, D_DIL)

    qn = to_bhsd(heads(q_n, N_NA_HEADS))
    kn = to_bhsd(heads(k_n, N_NA_HEADS))
    vn = to_bhsd(heads(v_n, N_NA_HEADS))
    y_na = neighbourhood_attention(qn, kn, vn, rpb).astype(x.dtype)
    y_na = y_na.transpose(0, 2, 1, 3).reshape(b, s, D_NA)

    y_conv = short_conv_mixer(gate_b, gate_c, h_in, conv_w)

    mixed = jnp.concatenate([y_dil, y_na, y_conv], axis=-1) @ w_out
    x = x + rmsnorm(mixed, g_post_mix)

    hf = rmsnorm(x, g_pre_ffn)
    f = (jax.nn.silu(hf @ w_gate) * (hf @ w_up)) @ w_down
    return x + rmsnorm(f, g_post_ffn)


def setup_inputs(seed: int = 0) -> dict:
    key = jax.random.key(seed)
    ks = jax.random.split(key, 14)
    nrm = lambda k, shape, scale: jax.random.normal(k, shape, jnp.float32) * scale
    gain = lambda k: 1.0 + 0.05 * jax.random.normal(k, (DEPTH, D_MODEL), jnp.float32)
    return {
        'x_prompt': nrm(ks[0], (BATCH, SEQ, D_MODEL), 1.0),
        'x_sample': nrm(ks[1], (DEC_BATCH, DEC_SEQ, D_MODEL), 1.0),
        'g_pre_mix': gain(ks[2]),
        'w_in': nrm(ks[3], (DEPTH, D_MODEL, D_IN), D_MODEL ** -0.5),
        'conv_w': nrm(ks[4], (DEPTH, CONV_WIDTH, D_CONV), CONV_WIDTH ** -0.5),
        'rpb': nrm(ks[5], (DEPTH, N_NA_HEADS, 2 * NA_ROWS - 1, 2 * NA_COLS - 1), 0.2),
        'w_out': nrm(ks[6], (DEPTH, D_MIX, D_MODEL), D_MIX ** -0.5),
        'g_post_mix': gain(ks[7]),
        'g_pre_ffn': gain(ks[8]),
        'w_gate': nrm(ks[9], (DEPTH, D_MODEL, D_FF), D_MODEL ** -0.5),
        'w_up': nrm(ks[10], (DEPTH, D_MODEL, D_FF), D_MODEL ** -0.5),
        'w_down': nrm(ks[11], (DEPTH, D_FF, D_MODEL), D_FF ** -0.5),
        'g_post_ffn': gain(ks[12]),
    }


def reference(x_prompt, x_sample, g_pre_mix, w_in, conv_w, rpb, w_out, g_post_mix,
              g_pre_ffn, w_gate, w_up, w_down, g_post_ffn):
    y_prompt = x_prompt
    y_sample = x_sample
    for l in range(DEPTH):
        y_prompt = encoder_layer(y_prompt, g_pre_mix[l], w_in[l], conv_w[l], rpb[l], w_out[l],
                                 g_post_mix[l], g_pre_ffn[l], w_gate[l], w_up[l], w_down[l],
                                 g_post_ffn[l])
        y_sample = encoder_layer(y_sample, g_pre_mix[l], w_in[l], conv_w[l], rpb[l], w_out[l],
                                 g_post_mix[l], g_pre_ffn[l], w_gate[l], w_up[l], w_down[l],
                                 g_post_ffn[l])
    return (y_prompt, y_sample)
```

```python
import functools

import numpy as np
import jax
import jax.numpy as jnp
from jax import lax
from jax.experimental import pallas as pl
from jax.experimental.pallas import tpu as pltpu

F32 = jnp.float32
BF16 = jnp.bfloat16

D_MODEL = 1024
HEAD_DIM = 64
D_DIL = 384
D_NA = 256
D_CONV = 384
D_IN = 3072
D_FF = 2816
DIL_BRANCHES = ((128, 1), (512, 4), (2048, 16))
DIL_BLOCK = 128
DIL_HALF = 64
GRID_W = 64
NA_ROWS = 8
NA_COLS = 16
NA_QBLOCK = 128
NA_KWIN = 640
ROPE_THETA = 10000.0
RMS_EPS = 1e-6
NEG_BIG = -1e30

LANES = 128
TOKEN_TILE = 512
FF_CHUNK = 1408
VMEM_LIMIT = 56 * 1024 * 1024


def _rms(x, g):
    ms = jnp.mean(x * x, axis=-1, keepdims=True)
    return x * lax.rsqrt(ms + RMS_EPS) * g


def _inproj_kernel(x_ref, g_ref, w_ref, cos_ref, sina_ref, sinb_ref,
                   qd_ref, kd_ref, vd_ref, qn_ref, kn_ref, vn_ref, gb_ref, u_ref):
    hn = _rms(x_ref[...], g_ref[...]).astype(BF16)

    def proj(lo, width):
        return jnp.dot(hn, w_ref[:, lo:lo + width], preferred_element_type=F32)

    cos = cos_ref[...]
    sina = sina_ref[...]
    sinb = sinb_ref[...]

    def rope(t):
        return t * cos + pltpu.roll(t, 96, 1) * sina + pltpu.roll(t, 32, 1) * sinb

    scale = HEAD_DIM ** -0.5
    for j in range(D_DIL // LANES):
        sl = slice(j * LANES, (j + 1) * LANES)
        qd_ref[:, sl] = rope(proj(j * LANES, LANES)) * scale
        kd_ref[:, sl] = rope(proj(D_DIL + j * LANES, LANES))
    vd_ref[...] = proj(2 * D_DIL, D_DIL)
    base = 3 * D_DIL
    qn_ref[...] = (proj(base, D_NA) * scale).astype(BF16)
    kn_ref[...] = proj(base + D_NA, D_NA).astype(BF16)
    vn_ref[...] = proj(base + 2 * D_NA, D_NA).astype(BF16)
    base += 3 * D_NA
    gb_ref[...] = proj(base, D_CONV)
    u_ref[...] = proj(base + D_CONV, D_CONV) * proj(base + 2 * D_CONV, D_CONV)


def _inproj(x2, g, w_bf, cos_t, sina_t, sinb_t, seq):
    t = x2.shape[0]
    tm = TOKEN_TILE
    tps = seq // tm
    row = lambda w: pl.BlockSpec((tm, w), lambda i: (i, 0))
    tab = pl.BlockSpec((tm, LANES), lambda i: (i % tps, 0))
    out_shape = (
        jax.ShapeDtypeStruct((t, D_DIL), F32), jax.ShapeDtypeStruct((t, D_DIL), F32),
        jax.ShapeDtypeStruct((t, D_DIL), F32),
        jax.ShapeDtypeStruct((t, D_NA), BF16), jax.ShapeDtypeStruct((t, D_NA), BF16),
        jax.ShapeDtypeStruct((t, D_NA), BF16),
        jax.ShapeDtypeStruct((t, D_CONV), F32), jax.ShapeDtypeStruct((t, D_CONV), F32),
    )
    return pl.pallas_call(
        _inproj_kernel,
        grid=(t // tm,),
        in_specs=[row(D_MODEL),
                  pl.BlockSpec((1, D_MODEL), lambda i: (0, 0)),
                  pl.BlockSpec((D_MODEL, D_IN), lambda i: (0, 0), pipeline_mode=pl.Buffered(1)),
                  tab, tab, tab],
        out_specs=(row(D_DIL), row(D_DIL), row(D_DIL), row(D_NA), row(D_NA), row(D_NA),
                   row(D_CONV), row(D_CONV)),
        out_shape=out_shape,
        compiler_params=pltpu.CompilerParams(dimension_semantics=("parallel",),
                                             vmem_limit_bytes=VMEM_LIMIT),
        name="inproj",
    )(x2, g, w_bf, cos_t, sina_t, sinb_t)


def _dil_kernel(q_ref, k_ref, v_ref, bias_ref, o_ref,
                qs0, qs1, ks, vs, acc, m_s, l_s, *, seq):
    lane = lax.broadcasted_iota(jnp.int32, (DIL_BLOCK, LANES), 1)
    is_h0 = lane < HEAD_DIM
    zeros_pad = jnp.zeros((DIL_HALF, LANES), BF16)
    nt = (((1,), (1,)), ((), ()))

    for bi, (_, d) in enumerate(DIL_BRANCHES):
        sub = seq // d
        nb = sub // DIL_BLOCK
        subp = sub + 2 * DIL_HALF

        for r in range(d):
            for buf in (ks, vs):
                buf[r * subp:r * subp + DIL_HALF, :] = zeros_pad
                buf[r * subp + DIL_HALF + sub:(r + 1) * subp, :] = zeros_pad

        def deinterleave(j, carry, d=d, nb=nb, subp=subp):
            r = j // nb
            c = j - r * nb
            src = pl.ds(r + c * (DIL_BLOCK * d), DIL_BLOCK, stride=d)
            qf = q_ref[src, :]
            dq = pl.ds(pl.multiple_of(j * DIL_BLOCK, DIL_BLOCK), DIL_BLOCK)
            qs0[dq, :] = jnp.where(is_h0, qf, 0.0).astype(BF16)
            qs1[dq, :] = jnp.where(is_h0, 0.0, qf).astype(BF16)
            dk = pl.ds(pl.multiple_of(r * subp + DIL_HALF + c * DIL_BLOCK, DIL_HALF), DIL_BLOCK)
            ks[dk, :] = k_ref[src, :].astype(BF16)
            vs[dk, :] = v_ref[src, :].astype(BF16)
            return carry

        lax.fori_loop(0, d * nb, deinterleave, 0)

        def block(j, carry, d=d, nb=nb, subp=subp, first=(bi == 0)):
            r = j // nb
            n = j - r * nb
            dq = pl.ds(pl.multiple_of(j * DIL_BLOCK, DIL_BLOCK), DIL_BLOCK)
            kwin = pl.ds(pl.multiple_of(r * subp + n * DIL_BLOCK, DIL_BLOCK), 2 * DIL_BLOCK)
            kw = ks[kwin, :]
            vw = vs[kwin, :]
            variant = jnp.where(n == 0, 1, 0) + jnp.where(n == nb - 1, 2, 0)
            bias = bias_ref[variant]
            parts = []
            for qs in (qs0, qs1):
                s = lax.dot_general(qs[dq, :], kw, nt, preferred_element_type=F32) + bias
                m = jnp.max(s, axis=-1, keepdims=True)
                p = jnp.exp(s - m)
                l = jnp.sum(p, axis=-1, keepdims=True)
                o = jnp.dot(p.astype(BF16), vw, preferred_element_type=F32)
                parts.append((o, m, l))
            o_b = jnp.where(is_h0, parts[0][0], parts[1][0])
            m_b = jnp.where(is_h0, parts[0][1], parts[1][1])
            l_b = jnp.where(is_h0, parts[0][2], parts[1][2])
            rows = pl.ds(r + n * (DIL_BLOCK * d), DIL_BLOCK, stride=d)
            if first:
                acc[rows, :] = o_b
                m_s[rows, :] = m_b
                l_s[rows, :] = l_b
            else:
                m_old = m_s[rows, :]
                m_new = jnp.maximum(m_old, m_b)
                a = jnp.exp(m_old - m_new)
                b = jnp.exp(m_b - m_new)
                acc[rows, :] = acc[rows, :] * a + o_b * b
                l_s[rows, :] = l_s[rows, :] * a + l_b * b
                m_s[rows, :] = m_new
            return carry

        lax.fori_loop(0, d * nb, block, 0)

    def finish(c, carry):
        rows = pl.ds(pl.multiple_of(c * 256, 256), 256)
        o_ref[rows, :] = (acc[rows, :] / l_s[rows, :]).astype(o_ref.dtype)
        return carry

    lax.fori_loop(0, seq // 256, finish, 0)


def _dil_bias():
    q = np.arange(DIL_BLOCK)[:, None]
    k = np.arange(2 * DIL_BLOCK)[None, :]
    band = np.abs(k - DIL_HALF - q) <= DIL_HALF
    first = k >= DIL_HALF
    last = k < DIL_HALF + DIL_BLOCK
    masks = np.stack([band, band & first, band & last, band & first & last])
    return np.where(masks, 0.0, NEG_BIG).astype(np.float32)


def _dilated_attention(qd, kd, vd, seq):
    b = qd.shape[0]
    max_rows = max(seq + 2 * DIL_HALF * d for _, d in DIL_BRANCHES)
    blk = pl.BlockSpec((None, seq, LANES), lambda i, h: (i, 0, h))
    return pl.pallas_call(
        functools.partial(_dil_kernel, seq=seq),
        grid=(b, D_DIL // LANES),
        in_specs=[blk, blk, blk,
                  pl.BlockSpec((4, DIL_BLOCK, 2 * DIL_BLOCK), lambda i, h: (0, 0, 0))],
        out_specs=blk,
        out_shape=jax.ShapeDtypeStruct((b, seq, D_DIL), BF16),
        scratch_shapes=[pltpu.VMEM((seq, LANES), BF16), pltpu.VMEM((seq, LANES), BF16),
                        pltpu.VMEM((max_rows, LANES), BF16), pltpu.VMEM((max_rows, LANES), BF16),
                        pltpu.VMEM((seq, LANES), F32), pltpu.VMEM((seq, LANES), F32),
                        pltpu.VMEM((seq, LANES), F32)],
        compiler_params=pltpu.CompilerParams(dimension_semantics=("parallel", "parallel"),
                                             vmem_limit_bytes=VMEM_LIMIT),
        name="dilated_attention",
    )(qd, kd, vd, jnp.asarray(_dil_bias()))


def _na_kernel(q_ref, k_ref, v_ref, bias_ref, hmask_ref, o_ref, *, seq):
    nb = seq // NA_QBLOCK
    lane = lax.broadcasted_iota(jnp.int32, (NA_QBLOCK, LANES), 1)
    is_h0 = lane < HEAD_DIM
    nt = (((1,), (1,)), ((), ()))

    def block(i, carry):
        ws = pl.multiple_of(jnp.clip(i * NA_QBLOCK - 256, 0, seq - NA_KWIN), LANES)
        variant = jnp.where(i < 2, i, jnp.where(i >= nb - 2, i - (nb - 5), 2))
        rows = pl.ds(pl.multiple_of(i * NA_QBLOCK, NA_QBLOCK), NA_QBLOCK)
        qb = q_ref[rows, :]
        kw = k_ref[pl.ds(ws, NA_KWIN), :]
        vw = v_ref[pl.ds(ws, NA_KWIN), :]
        outs = []
        for h in range(2):
            qm = qb * hmask_ref[h:h + 1, :]
            s = lax.dot_general(qm, kw, nt, preferred_element_type=F32) + bias_ref[h, variant]
            m = jnp.max(s, axis=-1, keepdims=True)
            p = jnp.exp(s - m)
            l = jnp.sum(p, axis=-1, keepdims=True)
            o = jnp.dot(p.astype(BF16), vw, preferred_element_type=F32)
            outs.append(o / l)
        o_ref[rows, :] = jnp.where(is_h0, outs[0], outs[1]).astype(o_ref.dtype)
        return carry

    lax.fori_loop(0, nb, block, 0)


def _na_tables(seq):
    rows = seq // GRID_W
    nb = seq // NA_QBLOCK
    assert rows >= 10 and nb >= 5
    reps = np.array([0, 1, 2, nb - 2, nb - 1])
    ws = np.clip(reps * NA_QBLOCK - 256, 0, seq - NA_KWIN)
    qtok = reps[:, None] * NA_QBLOCK + np.arange(NA_QBLOCK)[None, :]
    ktok = ws[:, None] + np.arange(NA_KWIN)[None, :]
    qr, qc = qtok // GRID_W, qtok % GRID_W
    kr, kc = ktok // GRID_W, ktok % GRID_W
    row_start = np.clip(qr - NA_ROWS // 2, 0, rows - NA_ROWS)
    col_start = np.clip(qc - NA_COLS // 2, 0, GRID_W - NA_COLS)
    kr_, kc_ = kr[:, None, :], kc[:, None, :]
    valid = ((kr_ >= row_start[:, :, None]) & (kr_ < row_start[:, :, None] + NA_ROWS)
             & (kc_ >= col_start[:, :, None]) & (kc_ < col_start[:, :, None] + NA_COLS))
    ri = np.clip(kr_ - qr[:, :, None] + NA_ROWS - 1, 0, 2 * NA_ROWS - 2)
    ci = np.clip(kc_ - qc[:, :, None] + NA_COLS - 1, 0, 2 * NA_COLS - 2)
    return ri, ci, valid


def _neighbourhood_attention(qn, kn, vn, rpb_l, seq):
    b = qn.shape[0]
    ri, ci, valid = _na_tables(seq)
    bias = jnp.where(valid[None], rpb_l[:, ri, ci].astype(F32), NEG_BIG)
    hmask = jnp.asarray((np.arange(LANES)[None, :] // HEAD_DIM == np.arange(2)[:, None]), BF16)
    blk = pl.BlockSpec((None, seq, LANES), lambda i, h: (i, 0, h))
    return pl.pallas_call(
        functools.partial(_na_kernel, seq=seq),
        grid=(b, D_NA // LANES),
        in_specs=[blk, blk, blk,
                  pl.BlockSpec((2, 5, NA_QBLOCK, NA_KWIN), lambda i, h: (h, 0, 0, 0)),
                  pl.BlockSpec((2, LANES), lambda i, h: (0, 0))],
        out_specs=blk,
        out_shape=jax.ShapeDtypeStruct((b, seq, D_NA), BF16),
        compiler_params=pltpu.CompilerParams(dimension_semantics=("parallel", "parallel"),
                                             vmem_limit_bytes=VMEM_LIMIT),
        name="neighbourhood_attention",
    )(qn, kn, vn, bias, hmask)


def _mix_kernel(x_ref, yd_ref, yn_ref, gb_ref, u_ref, up_ref, un_ref, cw_ref, w_ref, g_ref,
                o_ref, ubuf, *, tiles_per_seq):
    tm = x_ref.shape[0]
    i = pl.program_id(0)
    pos = i % tiles_per_seq
    keep_prev = jnp.where(pos == 0, 0.0, 1.0)
    keep_next = jnp.where(pos == tiles_per_seq - 1, 0.0, 1.0)
    ubuf[0:8, :] = up_ref[...] * keep_prev
    ubuf[8:8 + tm, :] = u_ref[...]
    ubuf[8 + tm:16 + tm, :] = un_ref[...] * keep_next
    conv = (cw_ref[0:1, :] * ubuf[7:7 + tm, :] + cw_ref[1:2, :] * ubuf[8:8 + tm, :]
            + cw_ref[2:3, :] * ubuf[9:9 + tm, :])
    y_conv = (gb_ref[...] * conv).astype(BF16)
    mixed = (jnp.dot(yd_ref[...], w_ref[0:D_DIL, :], preferred_element_type=F32)
             + jnp.dot(yn_ref[...], w_ref[D_DIL:D_DIL + D_NA, :], preferred_element_type=F32)
             + jnp.dot(y_conv, w_ref[D_DIL + D_NA:, :], preferred_element_type=F32))
    o_ref[...] = x_ref[...] + _rms(mixed, g_ref[...])


def _mix(x2, yd, yn, gb, u, conv_w, w_bf, g, seq):
    t = x2.shape[0]
    tm = TOKEN_TILE
    sub = tm // 8
    last = t // 8 - 1
    row = lambda w: pl.BlockSpec((tm, w), lambda i: (i, 0))
    return pl.pallas_call(
        functools.partial(_mix_kernel, tiles_per_seq=seq // tm),
        grid=(t // tm,),
        in_specs=[row(D_MODEL), row(D_DIL), row(D_NA), row(D_CONV), row(D_CONV),
                  pl.BlockSpec((8, D_CONV), lambda i: (jnp.maximum(i * sub - 1, 0), 0)),
                  pl.BlockSpec((8, D_CONV), lambda i: (jnp.minimum((i + 1) * sub, last), 0)),
                  pl.BlockSpec((3, D_CONV), lambda i: (0, 0)),
                  pl.BlockSpec((D_MODEL, D_MODEL), lambda i: (0, 0), pipeline_mode=pl.Buffered(1)),
                  pl.BlockSpec((1, D_MODEL), lambda i: (0, 0))],
        out_specs=row(D_MODEL),
        out_shape=jax.ShapeDtypeStruct((t, D_MODEL), F32),
        scratch_shapes=[pltpu.VMEM((tm + 16, D_CONV), F32)],
        compiler_params=pltpu.CompilerParams(dimension_semantics=("parallel",),
                                             vmem_limit_bytes=VMEM_LIMIT),
        name="mix_out",
    )(x2, yd, yn, gb, u, u, u, conv_w, w_bf, g)


def _ffn_kernel(x_ref, g1_ref, wg_ref, wu_ref, wd_ref, g2_ref, o_ref):
    x = x_ref[...]
    hf = _rms(x, g1_ref[...]).astype(BF16)
    f = None
    for c in range(0, D_FF, FF_CHUNK):
        gate = jnp.dot(hf, wg_ref[:, c:c + FF_CHUNK], preferred_element_type=F32)
        up = jnp.dot(hf, wu_ref[:, c:c + FF_CHUNK], preferred_element_type=F32)
        act = (gate * (1.0 / (1.0 + jnp.exp(-gate))) * up).astype(BF16)
        part = jnp.dot(act, wd_ref[c:c + FF_CHUNK, :], preferred_element_type=F32)
        f = part if f is None else f + part
    o_ref[...] = x + _rms(f, g2_ref[...])


def _ffn(x2, g1, wg_bf, wu_bf, wd_bf, g2):
    t = x2.shape[0]
    tm = TOKEN_TILE
    row = pl.BlockSpec((tm, D_MODEL), lambda i: (i, 0))
    vec = pl.BlockSpec((1, D_MODEL), lambda i: (0, 0))
    resident = lambda shape: pl.BlockSpec(shape, lambda i: (0, 0), pipeline_mode=pl.Buffered(1))
    return pl.pallas_call(
        _ffn_kernel,
        grid=(t // tm,),
        in_specs=[row, vec, resident((D_MODEL, D_FF)), resident((D_MODEL, D_FF)),
                  resident((D_FF, D_MODEL)), vec],
        out_specs=row,
        out_shape=jax.ShapeDtypeStruct((t, D_MODEL), F32),
        compiler_params=pltpu.CompilerParams(dimension_semantics=("parallel",),
                                             vmem_limit_bytes=VMEM_LIMIT),
        name="ffn",
    )(x2, g1, wg_bf, wu_bf, wd_bf, g2)


def _rope_tables(seq):
    pos = jnp.arange(seq, dtype=F32)
    inv = 1.0 / (ROPE_THETA ** (jnp.arange(0, HEAD_DIM, 2, dtype=F32) / HEAD_DIM))
    ang = pos[:, None] * inv[None, :]
    reps = LANES // (HEAD_DIM // 2)
    cos_t = jnp.tile(jnp.cos(ang), (1, reps))
    sin_t = jnp.tile(jnp.sin(ang), (1, reps))
    first_half = (np.arange(LANES) % HEAD_DIM) < HEAD_DIM // 2
    sina_t = jnp.where(first_half[None, :], -sin_t, 0.0)
    sinb_t = jnp.where(first_half[None, :], 0.0, sin_t)
    return cos_t, sina_t, sinb_t


def _layer(x, tables, p):
    b, seq, _ = x.shape
    assert seq % 2048 == 0 and seq % TOKEN_TILE == 0
    x2 = x.reshape(b * seq, D_MODEL)
    qd, kd, vd, qn, kn, vn, gb, u = _inproj(x2, p["g_pre_mix"], p["w_in"], *tables, seq)
    to3 = lambda a: a.reshape(b, seq, a.shape[-1])
    yd = _dilated_attention(to3(qd), to3(kd), to3(vd), seq).reshape(b * seq, D_DIL)
    yn = _neighbourhood_attention(to3(qn), to3(kn), to3(vn), p["rpb"], seq).reshape(b * seq, D_NA)
    x1 = _mix(x2, yd, yn, gb, u, p["conv_w"], p["w_out"], p["g_post_mix"], seq)
    x2 = _ffn(x1, p["g_pre_ffn"], p["w_gate"], p["w_up"], p["w_down"], p["g_post_ffn"])
    return x2.reshape(b, seq, D_MODEL)


def kernel(x_prompt, x_sample, g_pre_mix, w_in, conv_w, rpb, w_out, g_post_mix, g_pre_ffn,
           w_gate, w_up, w_down, g_post_ffn):
    depth = w_in.shape[0]
    streams = [x_prompt, x_sample]
    tables = [_rope_tables(x.shape[1]) for x in streams]
    for l in range(depth):
        p = dict(
            g_pre_mix=g_pre_mix[l][None, :], w_in=w_in[l].astype(BF16), conv_w=conv_w[l],
            rpb=rpb[l], w_out=w_out[l].astype(BF16), g_post_mix=g_post_mix[l][None, :],
            g_pre_ffn=g_pre_ffn[l][None, :], w_gate=w_gate[l].astype(BF16),
            w_up=w_up[l].astype(BF16), w_down=w_down[l].astype(BF16),
            g_post_ffn=g_post_ffn[l][None, :])
        streams = [_layer(x, t, p) for x, t in zip(streams, tables)]
    return tuple(streams)
```

```python
import functools

import numpy as np
import jax
import jax.numpy as jnp
from jax import lax
from jax.experimental import pallas as pl
from jax.experimental.pallas import tpu as pltpu

F32 = jnp.float32
BF16 = jnp.bfloat16

D_MODEL = 1024
HEAD_DIM = 64
D_DIL = 384
D_NA = 256
D_CONV = 384
D_IN = 3072
D_FF = 2816
DIL_BRANCHES = ((128, 1), (512, 4), (2048, 16))
DIL_BLOCK = 128
DIL_HALF = 64
DIL_UNROLL = 4
GRID_W = 64
NA_ROWS = 8
NA_COLS = 16
NA_QBLOCK = 128
NA_KWIN = 640
NA_UNROLL = 2
ROPE_THETA = 10000.0
RMS_EPS = 1e-6
NEG_BIG = -1e30

LANES = 128
TOKEN_TILE = 512
FF_CHUNK = 1408
VMEM_LIMIT = 56 * 1024 * 1024

_NT = (((1,), (1,)), ((), ()))


def _rms(x, g):
    ms = jnp.mean(x * x, axis=-1, keepdims=True)
    return x * lax.rsqrt(ms + RMS_EPS) * g


def _inproj_kernel(x_ref, g_ref, w_ref, cos_ref, sina_ref, sinb_ref,
                   qd_ref, kd_ref, vd_ref, qn_ref, kn_ref, vn_ref, gb_ref, u_ref):
    hn = _rms(x_ref[...], g_ref[...]).astype(BF16)

    def proj(lo, width):
        return jnp.dot(hn, w_ref[:, lo:lo + width], preferred_element_type=F32)

    cos = cos_ref[...]
    sina = sina_ref[...]
    sinb = sinb_ref[...]

    def rope(t):
        return t * cos + pltpu.roll(t, 96, 1) * sina + pltpu.roll(t, 32, 1) * sinb

    scale = HEAD_DIM ** -0.5
    for j in range(D_DIL // LANES):
        sl = slice(j * LANES, (j + 1) * LANES)
        qd_ref[:, sl] = rope(proj(j * LANES, LANES)) * scale
        kd_ref[:, sl] = rope(proj(D_DIL + j * LANES, LANES))
    vd_ref[...] = proj(2 * D_DIL, D_DIL)
    base = 3 * D_DIL
    qn_ref[...] = (proj(base, D_NA) * scale).astype(BF16)
    kn_ref[...] = proj(base + D_NA, D_NA).astype(BF16)
    vn_ref[...] = proj(base + 2 * D_NA, D_NA).astype(BF16)
    base += 3 * D_NA
    gb_ref[...] = proj(base, D_CONV)
    u_ref[...] = proj(base + D_CONV, D_CONV) * proj(base + 2 * D_CONV, D_CONV)


def _inproj(x2, g, w_bf, cos_t, sina_t, sinb_t, seq):
    t = x2.shape[0]
    tm = TOKEN_TILE
    tps = seq // tm
    row = lambda w: pl.BlockSpec((tm, w), lambda i: (i, 0))
    tab = pl.BlockSpec((tm, LANES), lambda i: (i % tps, 0))
    out_shape = (
        jax.ShapeDtypeStruct((t, D_DIL), F32), jax.ShapeDtypeStruct((t, D_DIL), F32),
        jax.ShapeDtypeStruct((t, D_DIL), F32),
        jax.ShapeDtypeStruct((t, D_NA), BF16), jax.ShapeDtypeStruct((t, D_NA), BF16),
        jax.ShapeDtypeStruct((t, D_NA), BF16),
        jax.ShapeDtypeStruct((t, D_CONV), F32), jax.ShapeDtypeStruct((t, D_CONV), F32),
    )
    return pl.pallas_call(
        _inproj_kernel,
        grid=(t // tm,),
        in_specs=[row(D_MODEL),
                  pl.BlockSpec((1, D_MODEL), lambda i: (0, 0)),
                  pl.BlockSpec((D_MODEL, D_IN), lambda i: (0, 0), pipeline_mode=pl.Buffered(1)),
                  tab, tab, tab],
        out_specs=(row(D_DIL), row(D_DIL), row(D_DIL), row(D_NA), row(D_NA), row(D_NA),
                   row(D_CONV), row(D_CONV)),
        out_shape=out_shape,
        compiler_params=pltpu.CompilerParams(dimension_semantics=("parallel",),
                                             vmem_limit_bytes=VMEM_LIMIT),
        name="inproj",
    )(x2, g, w_bf, cos_t, sina_t, sinb_t)


def _dil_kernel(q_ref, k_ref, v_ref, bias_ref, o_ref, qs, ks, vs, acc, m_s, l_s, *, seq):
    lane = lax.broadcasted_iota(jnp.int32, (DIL_BLOCK, LANES), 1)
    is_h0 = lane < HEAD_DIM
    zeros_pad = jnp.zeros((DIL_HALF, LANES), BF16)

    for bi, (_, d) in enumerate(sorted(DIL_BRANCHES, key=lambda wd: -wd[1])):
        sub = seq // d
        nb = sub // DIL_BLOCK
        subp = sub + 2 * DIL_HALF

        for r in range(d):
            for buf in (ks, vs):
                buf[r * subp:r * subp + DIL_HALF, :] = zeros_pad
                buf[r * subp + DIL_HALF + sub:(r + 1) * subp, :] = zeros_pad

        def deinterleave(j, carry, d=d, nb=nb, subp=subp):
            r = j // nb
            c = j - r * nb
            src = pl.ds(r + c * (DIL_BLOCK * d), DIL_BLOCK, stride=d)
            qf = q_ref[src, :]
            dq = pl.multiple_of(j * (2 * DIL_BLOCK), 2 * DIL_BLOCK)
            qs[pl.ds(dq, DIL_BLOCK), :] = jnp.where(is_h0, qf, 0.0).astype(BF16)
            qs[pl.ds(dq + DIL_BLOCK, DIL_BLOCK), :] = jnp.where(is_h0, 0.0, qf).astype(BF16)
            dk = pl.ds(pl.multiple_of(r * subp + DIL_HALF + c * DIL_BLOCK, DIL_HALF), DIL_BLOCK)
            ks[dk, :] = k_ref[src, :].astype(BF16)
            vs[dk, :] = v_ref[src, :].astype(BF16)
            return carry

        lax.fori_loop(0, d * nb, deinterleave, 0)

        def blocks(jj, carry, d=d, nb=nb, subp=subp, first=(bi == 0)):
            js = [jj * DIL_UNROLL + t for t in range(DIL_UNROLL)]
            rn = [(j // nb, j - (j // nb) * nb) for j in js]
            kwins = [pl.ds(pl.multiple_of(r * subp + n * DIL_BLOCK, DIL_BLOCK), 2 * DIL_BLOCK)
                     for r, n in rn]
            scores = []
            for j, (r, n), kwin in zip(js, rn, kwins):
                dq = pl.ds(pl.multiple_of(j * (2 * DIL_BLOCK), 2 * DIL_BLOCK), 2 * DIL_BLOCK)
                variant = jnp.where(n == 0, 1, 0) + jnp.where(n == nb - 1, 2, 0)
                s = lax.dot_general(qs[dq, :], ks[kwin, :], _NT, preferred_element_type=F32)
                scores.append(s + bias_ref[variant])
            stats = []
            for s in scores:
                m = jnp.max(s, axis=-1, keepdims=True)
                p = jnp.exp(s - m)
                stats.append((p.astype(BF16), m, jnp.sum(p, axis=-1, keepdims=True)))
            outs = [jnp.dot(p, vs[kwin, :], preferred_element_type=F32)
                    for (p, _, _), kwin in zip(stats, kwins)]
            for (r, n), o2, (_, m, l) in zip(rn, outs, stats):
                o_b = jnp.where(is_h0, o2[:DIL_BLOCK], o2[DIL_BLOCK:])
                m_b = jnp.where(is_h0, m[:DIL_BLOCK], m[DIL_BLOCK:])
                l_b = jnp.where(is_h0, l[:DIL_BLOCK], l[DIL_BLOCK:])
                rows = pl.ds(r + n * (DIL_BLOCK * d), DIL_BLOCK, stride=d)
                if first:
                    acc[rows, :] = o_b
                    m_s[rows, :] = m_b
                    l_s[rows, :] = l_b
                else:
                    m_old = m_s[rows, :]
                    m_new = jnp.maximum(m_old, m_b)
                    a = jnp.exp(m_old - m_new)
                    b = jnp.exp(m_b - m_new)
                    acc[rows, :] = acc[rows, :] * a + o_b * b
                    l_s[rows, :] = l_s[rows, :] * a + l_b * b
                    m_s[rows, :] = m_new
            return carry

        lax.fori_loop(0, d * nb // DIL_UNROLL, blocks, 0)

    def finish(c, carry):
        rows = pl.ds(pl.multiple_of(c * 256, 256), 256)
        o_ref[rows, :] = (acc[rows, :] / l_s[rows, :]).astype(o_ref.dtype)
        return carry

    lax.fori_loop(0, seq // 256, finish, 0)


def _dil_bias():
    q = np.arange(DIL_BLOCK)[:, None]
    k = np.arange(2 * DIL_BLOCK)[None, :]
    band = np.abs(k - DIL_HALF - q) <= DIL_HALF
    first = k >= DIL_HALF
    last = k < DIL_HALF + DIL_BLOCK
    masks = np.stack([band, band & first, band & last, band & first & last])
    bias = np.where(masks, 0.0, NEG_BIG).astype(np.float32)
    return np.concatenate([bias, bias], axis=1)


def _dilated_attention(qd, kd, vd, seq):
    b = qd.shape[0]
    assert (seq // DIL_BLOCK) % DIL_UNROLL == 0
    max_rows = max(seq + 2 * DIL_HALF * d for _, d in DIL_BRANCHES)
    blk = pl.BlockSpec((None, seq, LANES), lambda i, h: (i, 0, h))
    return pl.pallas_call(
        functools.partial(_dil_kernel, seq=seq),
        grid=(b, D_DIL // LANES),
        in_specs=[blk, blk, blk,
                  pl.BlockSpec((4, 2 * DIL_BLOCK, 2 * DIL_BLOCK), lambda i, h: (0, 0, 0))],
        out_specs=blk,
        out_shape=jax.ShapeDtypeStruct((b, seq, D_DIL), BF16),
        scratch_shapes=[pltpu.VMEM((2 * seq, LANES), BF16),
                        pltpu.VMEM((max_rows, LANES), BF16), pltpu.VMEM((max_rows, LANES), BF16),
                        pltpu.VMEM((seq, LANES), F32), pltpu.VMEM((seq, LANES), F32),
                        pltpu.VMEM((seq, LANES), F32)],
        compiler_params=pltpu.CompilerParams(dimension_semantics=("parallel", "parallel"),
                                             vmem_limit_bytes=VMEM_LIMIT),
        name="dilated_attention",
    )(qd, kd, vd, jnp.asarray(_dil_bias()))


def _na_kernel(q_ref, k_ref, v_ref, bias_ref, hmask_ref, o_ref, *, seq):
    nb = seq // NA_QBLOCK
    lane = lax.broadcasted_iota(jnp.int32, (NA_QBLOCK, LANES), 1)
    is_h0 = lane < HEAD_DIM

    def blocks(ii, carry):
        idx = [ii * NA_UNROLL + t for t in range(NA_UNROLL)]
        rows = [pl.ds(pl.multiple_of(i * NA_QBLOCK, NA_QBLOCK), NA_QBLOCK) for i in idx]
        wins = [pl.ds(pl.multiple_of(jnp.clip(i * NA_QBLOCK - 256, 0, seq - NA_KWIN), LANES),
                      NA_KWIN) for i in idx]
        scores = []
        for i, row, win in zip(idx, rows, wins):
            variant = jnp.where(i < 2, i, jnp.where(i >= nb - 2, i - (nb - 5), 2))
            qb = q_ref[row, :]
            q2 = jnp.concatenate([qb * hmask_ref[0:1, :], qb * hmask_ref[1:2, :]], axis=0)
            s = lax.dot_general(q2, k_ref[win, :], _NT, preferred_element_type=F32)
            scores.append(s + bias_ref[variant])
        probs = []
        for s in scores:
            m = jnp.max(s, axis=-1, keepdims=True)
            p = jnp.exp(s - m)
            probs.append((p.astype(BF16), jnp.sum(p, axis=-1, keepdims=True)))
        for row, win, (p, l) in zip(rows, wins, probs):
            o2 = jnp.dot(p, v_ref[win, :], preferred_element_type=F32) / l
            o_ref[row, :] = jnp.where(is_h0, o2[:NA_QBLOCK], o2[NA_QBLOCK:]).astype(o_ref.dtype)
        return carry

    lax.fori_loop(0, nb // NA_UNROLL, blocks, 0)


def _na_geometry(seq):
    rows = seq // GRID_W
    nb = seq // NA_QBLOCK
    assert rows >= 10 and nb >= 5 and nb % NA_UNROLL == 0
    reps = np.array([0, 1, 2, nb - 2, nb - 1])
    ws = np.clip(reps * NA_QBLOCK - 256, 0, seq - NA_KWIN)
    qtok = reps[:, None] * NA_QBLOCK + np.arange(NA_QBLOCK)[None, :]
    ktok = ws[:, None] + np.arange(NA_KWIN)[None, :]
    qr, qc = qtok // GRID_W, qtok % GRID_W
    kr, kc = ktok // GRID_W, ktok % GRID_W
    row_start = np.clip(qr - NA_ROWS // 2, 0, rows - NA_ROWS)
    col_start = np.clip(qc - NA_COLS // 2, 0, GRID_W - NA_COLS)
    kr_, kc_ = kr[:, None, :], kc[:, None, :]
    valid = ((kr_ >= row_start[:, :, None]) & (kr_ < row_start[:, :, None] + NA_ROWS)
             & (kc_ >= col_start[:, :, None]) & (kc_ < col_start[:, :, None] + NA_COLS))
    ri = np.clip(kr_ - qr[:, :, None] + NA_ROWS - 1, 0, 2 * NA_ROWS - 2)
    return ri[:, ::GRID_W, ::GRID_W], valid


def _na_bias(rpb_l, seq):
    ri_tile, valid = _na_geometry(seq)
    n_h = rpb_l.shape[0]
    edge = GRID_W - NA_COLS
    padded = jnp.pad(rpb_l.astype(F32), ((0, 0), (0, 0), (edge, edge)), mode="edge")
    toep = jnp.stack([padded[:, :, GRID_W - 1 - qc:2 * GRID_W - 1 - qc] for qc in range(GRID_W)],
                     axis=2)
    classes = []
    for v in range(ri_tile.shape[0]):
        qrows = [jnp.concatenate([toep[:, ri_tile[v, a, j]] for j in range(ri_tile.shape[2])],
                                 axis=-1) for a in range(ri_tile.shape[1])]
        classes.append(jnp.concatenate(qrows, axis=-2))
    bias = jnp.where(valid[None], jnp.stack(classes, axis=1), NEG_BIG)
    bias = bias.reshape(n_h // 2, 2, 5, NA_QBLOCK, NA_KWIN).transpose(0, 2, 1, 3, 4)
    return bias.reshape(n_h // 2, 5, 2 * NA_QBLOCK, NA_KWIN)


def _neighbourhood_attention(qn, kn, vn, bias, seq):
    b = qn.shape[0]
    hmask = jnp.asarray((np.arange(LANES)[None, :] // HEAD_DIM == np.arange(2)[:, None]), BF16)
    blk = pl.BlockSpec((None, seq, LANES), lambda i, h: (i, 0, h))
    return pl.pallas_call(
        functools.partial(_na_kernel, seq=seq),
        grid=(b, D_NA // LANES),
        in_specs=[blk, blk, blk,
                  pl.BlockSpec((None, 5, 2 * NA_QBLOCK, NA_KWIN), lambda i, h: (h, 0, 0, 0)),
                  pl.BlockSpec((2, LANES), lambda i, h: (0, 0))],
        out_specs=blk,
        out_shape=jax.ShapeDtypeStruct((b, seq, D_NA), BF16),
        compiler_params=pltpu.CompilerParams(dimension_semantics=("parallel", "parallel"),
                                             vmem_limit_bytes=VMEM_LIMIT),
        name="neighbourhood_attention",
    )(qn, kn, vn, bias, hmask)


def _mix_kernel(x_ref, yd_ref, yn_ref, gb_ref, u_ref, up_ref, un_ref, cw_ref, w_ref, g_ref,
                o_ref, ubuf, *, tiles_per_seq):
    tm = x_ref.shape[0]
    i = pl.program_id(0)
    pos = i % tiles_per_seq
    keep_prev = jnp.where(pos == 0, 0.0, 1.0)
    keep_next = jnp.where(pos == tiles_per_seq - 1, 0.0, 1.0)
    ubuf[0:8, :] = up_ref[...] * keep_prev
    ubuf[8:8 + tm, :] = u_ref[...]
    ubuf[8 + tm:16 + tm, :] = un_ref[...] * keep_next
    conv = (cw_ref[0:1, :] * ubuf[7:7 + tm, :] + cw_ref[1:2, :] * ubuf[8:8 + tm, :]
            + cw_ref[2:3, :] * ubuf[9:9 + tm, :])
    y_conv = (gb_ref[...] * conv).astype(BF16)
    mixed = (jnp.dot(yd_ref[...], w_ref[0:D_DIL, :], preferred_element_type=F32)
             + jnp.dot(yn_ref[...], w_ref[D_DIL:D_DIL + D_NA, :], preferred_element_type=F32)
             + jnp.dot(y_conv, w_ref[D_DIL + D_NA:, :], preferred_element_type=F32))
    o_ref[...] = x_ref[...] + _rms(mixed, g_ref[...])


def _mix(x2, yd, yn, gb, u, conv_w, w_bf, g, seq):
    t = x2.shape[0]
    tm = TOKEN_TILE
    sub = tm // 8
    last = t // 8 - 1
    row = lambda w: pl.BlockSpec((tm, w), lambda i: (i, 0))
    return pl.pallas_call(
        functools.partial(_mix_kernel, tiles_per_seq=seq // tm),
        grid=(t // tm,),
        in_specs=[row(D_MODEL), row(D_DIL), row(D_NA), row(D_CONV), row(D_CONV),
                  pl.BlockSpec((8, D_CONV), lambda i: (jnp.maximum(i * sub - 1, 0), 0)),
                  pl.BlockSpec((8, D_CONV), lambda i: (jnp.minimum((i + 1) * sub, last), 0)),
                  pl.BlockSpec((3, D_CONV), lambda i: (0, 0)),
                  pl.BlockSpec((D_MODEL, D_MODEL), lambda i: (0, 0), pipeline_mode=pl.Buffered(1)),
                  pl.BlockSpec((1, D_MODEL), lambda i: (0, 0))],
        out_specs=row(D_MODEL),
        out_shape=jax.ShapeDtypeStruct((t, D_MODEL), F32),
        scratch_shapes=[pltpu.VMEM((tm + 16, D_CONV), F32)],
        compiler_params=pltpu.CompilerParams(dimension_semantics=("parallel",),
                                             vmem_limit_bytes=VMEM_LIMIT),
        name="mix_out",
    )(x2, yd, yn, gb, u, u, u, conv_w, w_bf, g)


def _ffn_kernel(x_ref, g1_ref, wg_ref, wu_ref, wd_ref, g2_ref, o_ref):
    x = x_ref[...]
    hf = _rms(x, g1_ref[...]).astype(BF16)
    f = None
    for c in range(0, D_FF, FF_CHUNK):
        gate = jnp.dot(hf, wg_ref[:, c:c + FF_CHUNK], preferred_element_type=F32)
        up = jnp.dot(hf, wu_ref[:, c:c + FF_CHUNK], preferred_element_type=F32)
        act = (gate * (1.0 / (1.0 + jnp.exp(-gate))) * up).astype(BF16)
        part = jnp.dot(act, wd_ref[c:c + FF_CHUNK, :], preferred_element_type=F32)
        f = part if f is None else f + part
    o_ref[...] = x + _rms(f, g2_ref[...])


def _ffn(x2, g1, wg_bf, wu_bf, wd_bf, g2):
    t = x2.shape[0]
    tm = TOKEN_TILE
    row = pl.BlockSpec((tm, D_MODEL), lambda i: (i, 0))
    vec = pl.BlockSpec((1, D_MODEL), lambda i: (0, 0))
    resident = lambda shape: pl.BlockSpec(shape, lambda i: (0, 0), pipeline_mode=pl.Buffered(1))
    return pl.pallas_call(
        _ffn_kernel,
        grid=(t // tm,),
        in_specs=[row, vec, resident((D_MODEL, D_FF)), resident((D_MODEL, D_FF)),
                  resident((D_FF, D_MODEL)), vec],
        out_specs=row,
        out_shape=jax.ShapeDtypeStruct((t, D_MODEL), F32),
        compiler_params=pltpu.CompilerParams(dimension_semantics=("parallel",),
                                             vmem_limit_bytes=VMEM_LIMIT),
        name="ffn",
    )(x2, g1, wg_bf, wu_bf, wd_bf, g2)


def _rope_tables(seq):
    pos = jnp.arange(seq, dtype=F32)
    inv = 1.0 / (ROPE_THETA ** (jnp.arange(0, HEAD_DIM, 2, dtype=F32) / HEAD_DIM))
    ang = pos[:, None] * inv[None, :]
    reps = LANES // (HEAD_DIM // 2)
    cos_t = jnp.tile(jnp.cos(ang), (1, reps))
    sin_t = jnp.tile(jnp.sin(ang), (1, reps))
    first_half = (np.arange(LANES) % HEAD_DIM) < HEAD_DIM // 2
    sina_t = jnp.where(first_half[None, :], -sin_t, 0.0)
    sinb_t = jnp.where(first_half[None, :], 0.0, sin_t)
    return cos_t, sina_t, sinb_t


def _layer(x, tables, p):
    b, seq, _ = x.shape
    assert seq % 2048 == 0 and seq % TOKEN_TILE == 0
    x2 = x.reshape(b * seq, D_MODEL)
    qd, kd, vd, qn, kn, vn, gb, u = _inproj(x2, p["g_pre_mix"], p["w_in"], *tables, seq)
    to3 = lambda a: a.reshape(b, seq, a.shape[-1])
    yd = _dilated_attention(to3(qd), to3(kd), to3(vd), seq).reshape(b * seq, D_DIL)
    na_bias = _na_bias(p["rpb"], seq)
    yn = _neighbourhood_attention(to3(qn), to3(kn), to3(vn), na_bias, seq).reshape(b * seq, D_NA)
    x1 = _mix(x2, yd, yn, gb, u, p["conv_w"], p["w_out"], p["g_post_mix"], seq)
    x2 = _ffn(x1, p["g_pre_ffn"], p["w_gate"], p["w_up"], p["w_down"], p["g_post_ffn"])
    return x2.reshape(b, seq, D_MODEL)


def kernel(x_prompt, x_sample, g_pre_mix, w_in, conv_w, rpb, w_out, g_post_mix, g_pre_ffn,
           w_gate, w_up, w_down, g_post_ffn):
    depth = w_in.shape[0]
    streams = [x_prompt, x_sample]
    tables = [_rope_tables(x.shape[1]) for x in streams]
    for l in range(depth):
        p = dict(
            g_pre_mix=g_pre_mix[l][None, :], w_in=w_in[l].astype(BF16), conv_w=conv_w[l],
            rpb=rpb[l], w_out=w_out[l].astype(BF16), g_post_mix=g_post_mix[l][None, :],
            g_pre_ffn=g_pre_ffn[l][None, :], w_gate=w_gate[l].astype(BF16),
            w_up=w_up[l].astype(BF16), w_down=w_down[l].astype(BF16),
            g_post_ffn=g_post_ffn[l][None, :])
        streams = [_layer(x, t, p) for x, t in zip(streams, tables)]
    return tuple(streams)
```

```python
import functools

import numpy as np
import jax
import jax.numpy as jnp
from jax import lax
from jax.experimental import pallas as pl
from jax.experimental.pallas import tpu as pltpu

F32 = jnp.float32
BF16 = jnp.bfloat16

D_MODEL = 1024
HEAD_DIM = 64
D_DIL = 384
D_NA = 256
D_CONV = 384
D_IN = 3072
D_FF = 2816
DIL_BRANCHES = ((128, 1), (512, 4), (2048, 16))
DIL_BLOCK = 128
DIL_HALF = 64
DIL_UNROLL = 4
GRID_W = 64
NA_ROWS = 8
NA_COLS = 16
NA_QBLOCK = 128
NA_KWIN = 640
NA_UNROLL = 2
ROPE_THETA = 10000.0
RMS_EPS = 1e-6
NEG_BIG = -1e30
LOG2E = 1.4426950408889634
Q_SCALE = HEAD_DIM ** -0.5 * LOG2E

LANES = 128
MXU_WIDTH = 256
TOKEN_TILE = 512
INPROJ_CHUNK = 3 * MXU_WIDTH
FF_CHUNKS = ((0, 5 * MXU_WIDTH), (5 * MXU_WIDTH, D_FF))
VMEM_LIMIT = 56 * 1024 * 1024

_NT = (((1,), (1,)), ((), ()))


def _rms(x, g):
    ms = jnp.mean(x * x, axis=-1, keepdims=True)
    return x * lax.rsqrt(ms + RMS_EPS) * g


def _inproj_kernel(x_ref, g_ref, w_ref, cos_ref, sina_ref, sinb_ref,
                   qd_ref, kd_ref, vd_ref, qn_ref, kn_ref, vn_ref, gb_ref, u_ref):
    hn = _rms(x_ref[...], g_ref[...]).astype(BF16)

    groups = []
    for c in range(0, D_IN, INPROJ_CHUNK):
        r = jnp.dot(hn, w_ref[:, c:c + INPROJ_CHUNK], preferred_element_type=F32)
        groups += [r[:, k * LANES:(k + 1) * LANES] for k in range(INPROJ_CHUNK // LANES)]

    def take(n):
        out = groups[:n]
        del groups[:n]
        return out

    cos = cos_ref[...]
    sina = sina_ref[...]
    sinb = sinb_ref[...]

    def rope(t):
        return t * cos + pltpu.roll(t, 96, 1) * sina + pltpu.roll(t, 32, 1) * sinb

    def put(ref, parts):
        for j, part in enumerate(parts):
            ref[:, j * LANES:(j + 1) * LANES] = part.astype(ref.dtype)

    n_d, n_n, n_c = D_DIL // LANES, D_NA // LANES, D_CONV // LANES
    put(qd_ref, [rope(t) * Q_SCALE for t in take(n_d)])
    put(kd_ref, [rope(t) for t in take(n_d)])
    put(vd_ref, take(n_d))
    put(qn_ref, [t * Q_SCALE for t in take(n_n)])
    put(kn_ref, take(n_n))
    put(vn_ref, take(n_n))
    put(gb_ref, take(n_c))
    gate_c = take(n_c)
    put(u_ref, [c * h for c, h in zip(gate_c, take(n_c))])


def _inproj(x2, g, w_bf, cos_t, sina_t, sinb_t, seq):
    t = x2.shape[0]
    tm = TOKEN_TILE
    tps = seq // tm
    row = lambda w: pl.BlockSpec((tm, w), lambda i: (i, 0))
    tab = pl.BlockSpec((tm, LANES), lambda i: (i % tps, 0))
    out_shape = (
        jax.ShapeDtypeStruct((t, D_DIL), F32), jax.ShapeDtypeStruct((t, D_DIL), F32),
        jax.ShapeDtypeStruct((t, D_DIL), F32),
        jax.ShapeDtypeStruct((t, D_NA), BF16), jax.ShapeDtypeStruct((t, D_NA), BF16),
        jax.ShapeDtypeStruct((t, D_NA), BF16),
        jax.ShapeDtypeStruct((t, D_CONV), F32), jax.ShapeDtypeStruct((t, D_CONV), F32),
    )
    return pl.pallas_call(
        _inproj_kernel,
        grid=(t // tm,),
        in_specs=[row(D_MODEL),
                  pl.BlockSpec((1, D_MODEL), lambda i: (0, 0)),
                  pl.BlockSpec((D_MODEL, D_IN), lambda i: (0, 0), pipeline_mode=pl.Buffered(1)),
                  tab, tab, tab],
        out_specs=(row(D_DIL), row(D_DIL), row(D_DIL), row(D_NA), row(D_NA), row(D_NA),
                   row(D_CONV), row(D_CONV)),
        out_shape=out_shape,
        compiler_params=pltpu.CompilerParams(dimension_semantics=("parallel",),
                                             vmem_limit_bytes=VMEM_LIMIT),
        name="inproj",
    )(x2, g, w_bf, cos_t, sina_t, sinb_t)


def _dil_kernel(q_ref, k_ref, v_ref, bias_ref, o_ref, qs, ks, vs, acc, m_s, l_s, *, seq):
    lane = lax.broadcasted_iota(jnp.int32, (DIL_BLOCK, LANES), 1)
    is_h0 = lane < HEAD_DIM
    zeros_pad = jnp.zeros((DIL_HALF, LANES), BF16)

    for bi, (_, d) in enumerate(sorted(DIL_BRANCHES, key=lambda wd: -wd[1])):
        sub = seq // d
        nb = sub // DIL_BLOCK
        subp = sub + 2 * DIL_HALF

        for r in range(d):
            for buf in (ks, vs):
                buf[r * subp:r * subp + DIL_HALF, :] = zeros_pad
                buf[r * subp + DIL_HALF + sub:(r + 1) * subp, :] = zeros_pad

        def deinterleave(j, carry, d=d, nb=nb, subp=subp):
            r = j // nb
            c = j - r * nb
            src = pl.ds(r + c * (DIL_BLOCK * d), DIL_BLOCK, stride=d)
            qf = q_ref[src, :]
            dq = pl.multiple_of(j * (2 * DIL_BLOCK), 2 * DIL_BLOCK)
            qs[pl.ds(dq, DIL_BLOCK), :] = jnp.where(is_h0, qf, 0.0).astype(BF16)
            qs[pl.ds(dq + DIL_BLOCK, DIL_BLOCK), :] = jnp.where(is_h0, 0.0, qf).astype(BF16)
            dk = pl.ds(pl.multiple_of(r * subp + DIL_HALF + c * DIL_BLOCK, DIL_HALF), DIL_BLOCK)
            ks[dk, :] = k_ref[src, :].astype(BF16)
            vs[dk, :] = v_ref[src, :].astype(BF16)
            return carry

        lax.fori_loop(0, d * nb, deinterleave, 0)

        def blocks(jj, carry, d=d, nb=nb, subp=subp, first=(bi == 0)):
            js = [jj * DIL_UNROLL + t for t in range(DIL_UNROLL)]
            rn = [(j // nb, j - (j // nb) * nb) for j in js]
            kwins = [pl.ds(pl.multiple_of(r * subp + n * DIL_BLOCK, DIL_BLOCK), 2 * DIL_BLOCK)
                     for r, n in rn]
            scores = []
            for j, (r, n), kwin in zip(js, rn, kwins):
                dq = pl.ds(pl.multiple_of(j * (2 * DIL_BLOCK), 2 * DIL_BLOCK), 2 * DIL_BLOCK)
                variant = jnp.where(n == 0, 1, 0) + jnp.where(n == nb - 1, 2, 0)
                s = lax.dot_general(qs[dq, :], ks[kwin, :], _NT, preferred_element_type=F32)
                scores.append(s + bias_ref[variant])
            stats = []
            for s in scores:
                m = jnp.max(s, axis=-1, keepdims=True)
                p = jnp.exp2(s - m)
                stats.append((p.astype(BF16), m, jnp.sum(p, axis=-1, keepdims=True)))
            outs = [jnp.dot(p, vs[kwin, :], preferred_element_type=F32)
                    for (p, _, _), kwin in zip(stats, kwins)]
            for (r, n), o2, (_, m, l) in zip(rn, outs, stats):
                o_b = jnp.where(is_h0, o2[:DIL_BLOCK], o2[DIL_BLOCK:])
                m_b = jnp.where(is_h0, m[:DIL_BLOCK], m[DIL_BLOCK:])
                l_b = jnp.where(is_h0, l[:DIL_BLOCK], l[DIL_BLOCK:])
                rows = pl.ds(r + n * (DIL_BLOCK * d), DIL_BLOCK, stride=d)
                if first:
                    acc[rows, :] = o_b
                    m_s[rows, :] = m_b
                    l_s[rows, :] = l_b
                else:
                    m_old = m_s[rows, :]
                    m_new = jnp.maximum(m_old, m_b)
                    a = jnp.exp2(m_old - m_new)
                    b = jnp.exp2(m_b - m_new)
                    acc[rows, :] = acc[rows, :] * a + o_b * b
                    l_s[rows, :] = l_s[rows, :] * a + l_b * b
                    m_s[rows, :] = m_new
            return carry

        lax.fori_loop(0, d * nb // DIL_UNROLL, blocks, 0)

    def finish(c, carry):
        rows = pl.ds(pl.multiple_of(c * 256, 256), 256)
        o_ref[rows, :] = (acc[rows, :] / l_s[rows, :]).astype(o_ref.dtype)
        return carry

    lax.fori_loop(0, seq // 256, finish, 0)


def _dil_bias():
    q = np.arange(DIL_BLOCK)[:, None]
    k = np.arange(2 * DIL_BLOCK)[None, :]
    band = np.abs(k - DIL_HALF - q) <= DIL_HALF
    first = k >= DIL_HALF
    last = k < DIL_HALF + DIL_BLOCK
    masks = np.stack([band, band & first, band & last, band & first & last])
    bias = np.where(masks, 0.0, NEG_BIG).astype(np.float32)
    return np.concatenate([bias, bias], axis=1)


def _dilated_attention(qd, kd, vd, seq):
    b = qd.shape[0]
    assert (seq // DIL_BLOCK) % DIL_UNROLL == 0
    max_rows = max(seq + 2 * DIL_HALF * d for _, d in DIL_BRANCHES)
    blk = pl.BlockSpec((None, seq, LANES), lambda i, h: (i, 0, h))
    return pl.pallas_call(
        functools.partial(_dil_kernel, seq=seq),
        grid=(b, D_DIL // LANES),
        in_specs=[blk, blk, blk,
                  pl.BlockSpec((4, 2 * DIL_BLOCK, 2 * DIL_BLOCK), lambda i, h: (0, 0, 0))],
        out_specs=blk,
        out_shape=jax.ShapeDtypeStruct((b, seq, D_DIL), BF16),
        scratch_shapes=[pltpu.VMEM((2 * seq, LANES), BF16),
                        pltpu.VMEM((max_rows, LANES), BF16), pltpu.VMEM((max_rows, LANES), BF16),
                        pltpu.VMEM((seq, LANES), F32), pltpu.VMEM((seq, LANES), F32),
                        pltpu.VMEM((seq, LANES), F32)],
        compiler_params=pltpu.CompilerParams(dimension_semantics=("parallel", "parallel"),
                                             vmem_limit_bytes=VMEM_LIMIT),
        name="dilated_attention",
    )(qd, kd, vd, jnp.asarray(_dil_bias()))


def _na_kernel(q_ref, k_ref, v_ref, bias_ref, hmask_ref, o_ref, *, seq):
    nb = seq // NA_QBLOCK
    lane = lax.broadcasted_iota(jnp.int32, (NA_QBLOCK, LANES), 1)
    is_h0 = lane < HEAD_DIM

    def blocks(ii, carry):
        idx = [ii * NA_UNROLL + t for t in range(NA_UNROLL)]
        rows = [pl.ds(pl.multiple_of(i * NA_QBLOCK, NA_QBLOCK), NA_QBLOCK) for i in idx]
        wins = [pl.ds(pl.multiple_of(jnp.clip(i * NA_QBLOCK - 256, 0, seq - NA_KWIN), LANES),
                      NA_KWIN) for i in idx]
        scores = []
        for i, row, win in zip(idx, rows, wins):
            variant = jnp.where(i < 2, i, jnp.where(i >= nb - 2, i - (nb - 5), 2))
            qb = q_ref[row, :]
            q2 = jnp.concatenate([qb * hmask_ref[0:1, :], qb * hmask_ref[1:2, :]], axis=0)
            s = lax.dot_general(q2, k_ref[win, :], _NT, preferred_element_type=F32)
            scores.append(s + bias_ref[variant])
        probs = []
        for s in scores:
            m = jnp.max(s, axis=-1, keepdims=True)
            p = jnp.exp2(s - m)
            probs.append((p.astype(BF16), jnp.sum(p, axis=-1, keepdims=True)))
        for row, win, (p, l) in zip(rows, wins, probs):
            o2 = jnp.dot(p, v_ref[win, :], preferred_element_type=F32) / l
            o_ref[row, :] = jnp.where(is_h0, o2[:NA_QBLOCK], o2[NA_QBLOCK:]).astype(o_ref.dtype)
        return carry

    lax.fori_loop(0, nb // NA_UNROLL, blocks, 0)


def _na_geometry(seq):
    rows = seq // GRID_W
    nb = seq // NA_QBLOCK
    assert rows >= 10 and nb >= 5 and nb % NA_UNROLL == 0
    reps = np.array([0, 1, 2, nb - 2, nb - 1])
    ws = np.clip(reps * NA_QBLOCK - 256, 0, seq - NA_KWIN)
    qtok = reps[:, None] * NA_QBLOCK + np.arange(NA_QBLOCK)[None, :]
    ktok = ws[:, None] + np.arange(NA_KWIN)[None, :]
    qr, qc = qtok // GRID_W, qtok % GRID_W
    kr, kc = ktok // GRID_W, ktok % GRID_W
    row_start = np.clip(qr - NA_ROWS // 2, 0, rows - NA_ROWS)
    col_start = np.clip(qc - NA_COLS // 2, 0, GRID_W - NA_COLS)
    kr_, kc_ = kr[:, None, :], kc[:, None, :]
    valid = ((kr_ >= row_start[:, :, None]) & (kr_ < row_start[:, :, None] + NA_ROWS)
             & (kc_ >= col_start[:, :, None]) & (kc_ < col_start[:, :, None] + NA_COLS))
    ri = np.clip(kr_ - qr[:, :, None] + NA_ROWS - 1, 0, 2 * NA_ROWS - 2)
    return ri[:, ::GRID_W, ::GRID_W], valid


def _na_bias(rpb_l, seq):
    ri_tile, valid = _na_geometry(seq)
    n_h = rpb_l.shape[0]
    edge = GRID_W - NA_COLS
    padded = jnp.pad(rpb_l.astype(F32), ((0, 0), (0, 0), (edge, edge)), mode="edge")
    toep = jnp.stack([padded[:, :, GRID_W - 1 - qc:2 * GRID_W - 1 - qc] for qc in range(GRID_W)],
                     axis=2)
    classes = []
    for v in range(ri_tile.shape[0]):
        qrows = [jnp.concatenate([toep[:, ri_tile[v, a, j]] for j in range(ri_tile.shape[2])],
                                 axis=-1) for a in range(ri_tile.shape[1])]
        classes.append(jnp.concatenate(qrows, axis=-2))
    bias = jnp.where(valid[None], jnp.stack(classes, axis=1) * LOG2E, NEG_BIG)
    bias = bias.reshape(n_h // 2, 2, 5, NA_QBLOCK, NA_KWIN).transpose(0, 2, 1, 3, 4)
    return bias.reshape(n_h // 2, 5, 2 * NA_QBLOCK, NA_KWIN)


def _neighbourhood_attention(qn, kn, vn, bias, seq):
    b = qn.shape[0]
    hmask = jnp.asarray((np.arange(LANES)[None, :] // HEAD_DIM == np.arange(2)[:, None]), BF16)
    blk = pl.BlockSpec((None, seq, LANES), lambda i, h: (i, 0, h))
    return pl.pallas_call(
        functools.partial(_na_kernel, seq=seq),
        grid=(b, D_NA // LANES),
        in_specs=[blk, blk, blk,
                  pl.BlockSpec((None, 5, 2 * NA_QBLOCK, NA_KWIN), lambda i, h: (h, 0, 0, 0)),
                  pl.BlockSpec((2, LANES), lambda i, h: (0, 0))],
        out_specs=blk,
        out_shape=jax.ShapeDtypeStruct((b, seq, D_NA), BF16),
        compiler_params=pltpu.CompilerParams(dimension_semantics=("parallel", "parallel"),
                                             vmem_limit_bytes=VMEM_LIMIT),
        name="neighbourhood_attention",
    )(qn, kn, vn, bias, hmask)


def _mix_ffn_kernel(x_ref, yd_ref, yn_ref, gb_ref, u_ref, up_ref, un_ref, cw_ref, wo_ref,
                    g_mix_ref, g_pre_ref, wg_ref, wu_ref, wd_ref, g_post_ref, o_ref, ubuf,
                    *, tiles_per_seq):
    tm = x_ref.shape[0]
    pos = pl.program_id(0) % tiles_per_seq
    keep_prev = jnp.where(pos == 0, 0.0, 1.0)
    keep_next = jnp.where(pos == tiles_per_seq - 1, 0.0, 1.0)
    ubuf[0:8, :] = up_ref[...] * keep_prev
    ubuf[8:8 + tm, :] = u_ref[...]
    ubuf[8 + tm:16 + tm, :] = un_ref[...] * keep_next
    conv = (cw_ref[0:1, :] * ubuf[7:7 + tm, :] + cw_ref[1:2, :] * ubuf[8:8 + tm, :]
            + cw_ref[2:3, :] * ubuf[9:9 + tm, :])
    y_conv = (gb_ref[...] * conv).astype(BF16)
    cat = jnp.concatenate([yd_ref[...], yn_ref[...], y_conv], axis=-1)
    mixed = jnp.dot(cat, wo_ref[...], preferred_element_type=F32)
    x1 = x_ref[...] + _rms(mixed, g_mix_ref[...])

    hf = _rms(x1, g_pre_ref[...]).astype(BF16)
    f = None
    for lo, hi in FF_CHUNKS:
        gate = jnp.dot(hf, wg_ref[:, lo:hi], preferred_element_type=F32)
        up = jnp.dot(hf, wu_ref[:, lo:hi], preferred_element_type=F32)
        act = (gate * (1.0 / (1.0 + jnp.exp(-gate))) * up).astype(BF16)
        part = jnp.dot(act, wd_ref[lo:hi, :], preferred_element_type=F32)
        f = part if f is None else f + part
    o_ref[...] = x1 + _rms(f, g_post_ref[...])


def _mix_ffn(x2, yd, yn, gb, u, p, seq):
    t = x2.shape[0]
    tm = TOKEN_TILE
    sub = tm // 8
    last = t // 8 - 1
    row = lambda w: pl.BlockSpec((tm, w), lambda i: (i, 0))
    vec = pl.BlockSpec((1, D_MODEL), lambda i: (0, 0))
    resident = lambda shape: pl.BlockSpec(shape, lambda i: (0, 0), pipeline_mode=pl.Buffered(1))
    return pl.pallas_call(
        functools.partial(_mix_ffn_kernel, tiles_per_seq=seq // tm),
        grid=(t // tm,),
        in_specs=[row(D_MODEL), row(D_DIL), row(D_NA), row(D_CONV), row(D_CONV),
                  pl.BlockSpec((8, D_CONV), lambda i: (jnp.maximum(i * sub - 1, 0), 0)),
                  pl.BlockSpec((8, D_CONV), lambda i: (jnp.minimum((i + 1) * sub, last), 0)),
                  pl.BlockSpec((3, D_CONV), lambda i: (0, 0)),
                  resident((D_MODEL, D_MODEL)), vec, vec,
                  resident((D_MODEL, D_FF)), resident((D_MODEL, D_FF)), resident((D_FF, D_MODEL)),
                  vec],
        out_specs=row(D_MODEL),
        out_shape=jax.ShapeDtypeStruct((t, D_MODEL), F32),
        scratch_shapes=[pltpu.VMEM((tm + 16, D_CONV), F32)],
        compiler_params=pltpu.CompilerParams(dimension_semantics=("parallel",),
                                             vmem_limit_bytes=VMEM_LIMIT),
        name="mix_ffn",
    )(x2, yd, yn, gb, u, u, u, p["conv_w"], p["w_out"], p["g_post_mix"], p["g_pre_ffn"],
      p["w_gate"], p["w_up"], p["w_down"], p["g_post_ffn"])


def _rope_tables(seq):
    pos = jnp.arange(seq, dtype=F32)
    inv = 1.0 / (ROPE_THETA ** (jnp.arange(0, HEAD_DIM, 2, dtype=F32) / HEAD_DIM))
    ang = pos[:, None] * inv[None, :]
    reps = LANES // (HEAD_DIM // 2)
    cos_t = jnp.tile(jnp.cos(ang), (1, reps))
    sin_t = jnp.tile(jnp.sin(ang), (1, reps))
    first_half = (np.arange(LANES) % HEAD_DIM) < HEAD_DIM // 2
    sina_t = jnp.where(first_half[None, :], -sin_t, 0.0)
    sinb_t = jnp.where(first_half[None, :], 0.0, sin_t)
    return cos_t, sina_t, sinb_t


def _layer(x, tables, p):
    b, seq, _ = x.shape
    assert seq % 2048 == 0 and seq % TOKEN_TILE == 0
    x2 = x.reshape(b * seq, D_MODEL)
    qd, kd, vd, qn, kn, vn, gb, u = _inproj(x2, p["g_pre_mix"], p["w_in"], *tables, seq)
    to3 = lambda a: a.reshape(b, seq, a.shape[-1])
    yd = _dilated_attention(to3(qd), to3(kd), to3(vd), seq).reshape(b * seq, D_DIL)
    na_bias = _na_bias(p["rpb"], seq)
    yn = _neighbourhood_attention(to3(qn), to3(kn), to3(vn), na_bias, seq).reshape(b * seq, D_NA)
    out = _mix_ffn(x2, yd, yn, gb, u, p, seq)
    return out.reshape(b, seq, D_MODEL)


def kernel(x_prompt, x_sample, g_pre_mix, w_in, conv_w, rpb, w_out, g_post_mix, g_pre_ffn,
           w_gate, w_up, w_down, g_post_ffn):
    depth = w_in.shape[0]
    streams = [x_prompt, x_sample]
    tables = [_rope_tables(x.shape[1]) for x in streams]
    for l in range(depth):
        p = dict(
            g_pre_mix=g_pre_mix[l][None, :], w_in=w_in[l].astype(BF16), conv_w=conv_w[l],
            rpb=rpb[l], w_out=w_out[l].astype(BF16), g_post_mix=g_post_mix[l][None, :],
            g_pre_ffn=g_pre_ffn[l][None, :], w_gate=w_gate[l].astype(BF16),
            w_up=w_up[l].astype(BF16), w_down=w_down[l].astype(BF16),
            g_post_ffn=g_post_ffn[l][None, :])
        streams = [_layer(x, t, p) for x, t in zip(streams, tables)]
    return tuple(streams)
```

```python
import functools

import numpy as np
import jax
import jax.numpy as jnp
from jax import lax
from jax.experimental import pallas as pl
from jax.experimental.pallas import tpu as pltpu

F32 = jnp.float32
BF16 = jnp.bfloat16

D_MODEL = 1024
HEAD_DIM = 64
D_DIL = 384
D_NA = 256
D_CONV = 384
D_IN = 3072
D_FF = 2816
DIL_BRANCHES = ((128, 1), (512, 4), (2048, 16))
DIL_BLOCK = 128
DIL_HALF = 64
DIL_UNROLL = 8
GRID_W = 64
NA_ROWS = 8
NA_COLS = 16
NA_QBLOCK = 128
NA_KWIN = 640
NA_UNROLL = 4
ROPE_THETA = 10000.0
RMS_EPS = 1e-6
NEG_BIG = -1e30
LOG2E = 1.4426950408889634
Q_SCALE = HEAD_DIM ** -0.5 * LOG2E

LANES = 128
MXU_WIDTH = 256
TOKEN_TILE = 512
INPROJ_CHUNK = 3 * MXU_WIDTH
FF_CHUNKS = ((0, 5 * MXU_WIDTH), (5 * MXU_WIDTH, D_FF))
VMEM_LIMIT = 56 * 1024 * 1024

_NT = (((1,), (1,)), ((), ()))


def _rms(x, g):
    ms = jnp.mean(x * x, axis=-1, keepdims=True)
    return x * lax.rsqrt(ms + RMS_EPS) * g


def _inproj_kernel(x_ref, g_ref, w_ref, cos_ref, sina_ref, sinb_ref,
                   qd_ref, kd_ref, vd_ref, qn_ref, kn_ref, vn_ref, gb_ref, u_ref):
    hn = _rms(x_ref[...], g_ref[...]).astype(BF16)

    groups = []
    for c in range(0, D_IN, INPROJ_CHUNK):
        r = jnp.dot(hn, w_ref[:, c:c + INPROJ_CHUNK], preferred_element_type=F32)
        groups += [r[:, k * LANES:(k + 1) * LANES] for k in range(INPROJ_CHUNK // LANES)]

    def take(n):
        out = groups[:n]
        del groups[:n]
        return out

    cos = cos_ref[...]
    sina = sina_ref[...]
    sinb = sinb_ref[...]

    def rope(t):
        return t * cos + pltpu.roll(t, 96, 1) * sina + pltpu.roll(t, 32, 1) * sinb

    def put(ref, parts):
        for j, part in enumerate(parts):
            ref[:, j * LANES:(j + 1) * LANES] = part.astype(ref.dtype)

    n_d, n_n, n_c = D_DIL // LANES, D_NA // LANES, D_CONV // LANES
    put(qd_ref, [rope(t) * Q_SCALE for t in take(n_d)])
    put(kd_ref, [rope(t) for t in take(n_d)])
    put(vd_ref, take(n_d))
    put(qn_ref, [t * Q_SCALE for t in take(n_n)])
    put(kn_ref, take(n_n))
    put(vn_ref, take(n_n))
    put(gb_ref, take(n_c))
    gate_c = take(n_c)
    put(u_ref, [c * h for c, h in zip(gate_c, take(n_c))])


def _inproj(x2, g, w_bf, cos_t, sina_t, sinb_t, seq):
    t = x2.shape[0]
    tm = TOKEN_TILE
    tps = seq // tm
    row = lambda w: pl.BlockSpec((tm, w), lambda i: (i, 0))
    tab = pl.BlockSpec((tm, LANES), lambda i: (i % tps, 0))
    out_shape = (
        jax.ShapeDtypeStruct((t, D_DIL), F32), jax.ShapeDtypeStruct((t, D_DIL), F32),
        jax.ShapeDtypeStruct((t, D_DIL), F32),
        jax.ShapeDtypeStruct((t, D_NA), BF16), jax.ShapeDtypeStruct((t, D_NA), BF16),
        jax.ShapeDtypeStruct((t, D_NA), BF16),
        jax.ShapeDtypeStruct((t, D_CONV), F32), jax.ShapeDtypeStruct((t, D_CONV), F32),
    )
    return pl.pallas_call(
        _inproj_kernel,
        grid=(t // tm,),
        in_specs=[row(D_MODEL),
                  pl.BlockSpec((1, D_MODEL), lambda i: (0, 0)),
                  pl.BlockSpec((D_MODEL, D_IN), lambda i: (0, 0), pipeline_mode=pl.Buffered(1)),
                  tab, tab, tab],
        out_specs=(row(D_DIL), row(D_DIL), row(D_DIL), row(D_NA), row(D_NA), row(D_NA),
                   row(D_CONV), row(D_CONV)),
        out_shape=out_shape,
        compiler_params=pltpu.CompilerParams(dimension_semantics=("parallel",),
                                             vmem_limit_bytes=VMEM_LIMIT),
        name="inproj",
    )(x2, g, w_bf, cos_t, sina_t, sinb_t)


def _dil_kernel(q_ref, k_ref, v_ref, bias_ref, o_ref, qs, ks, vs, acc, m_s, l_s, tmp, *, seq):
    lane = lax.broadcasted_iota(jnp.int32, (DIL_BLOCK, LANES), 1)
    is_h0 = lane < HEAD_DIM
    zeros_pad = jnp.zeros((DIL_HALF, LANES), BF16)
    ones_blk = jnp.ones((2 * DIL_BLOCK, LANES), BF16)

    for bi, (_, d) in enumerate(sorted(DIL_BRANCHES, key=lambda wd: -wd[1])):
        sub = seq // d
        nb = sub // DIL_BLOCK
        subp = sub + 2 * DIL_HALF

        for r in range(d):
            for buf in (ks, vs):
                buf[r * subp:r * subp + DIL_HALF, :] = zeros_pad
                buf[r * subp + DIL_HALF + sub:(r + 1) * subp, :] = zeros_pad

        def deinterleave(j, carry, d=d, nb=nb, subp=subp):
            r = j // nb
            c = j - r * nb
            src = pl.ds(r + c * (DIL_BLOCK * d), DIL_BLOCK, stride=d)
            qf = q_ref[src, :]
            dq = pl.multiple_of(j * (2 * DIL_BLOCK), 2 * DIL_BLOCK)
            qs[pl.ds(dq, DIL_BLOCK), :] = jnp.where(is_h0, qf, 0.0).astype(BF16)
            qs[pl.ds(dq + DIL_BLOCK, DIL_BLOCK), :] = jnp.where(is_h0, 0.0, qf).astype(BF16)
            dk = pl.ds(pl.multiple_of(r * subp + DIL_HALF + c * DIL_BLOCK, DIL_HALF), DIL_BLOCK)
            ks[dk, :] = k_ref[src, :].astype(BF16)
            vs[dk, :] = v_ref[src, :].astype(BF16)
            return carry

        def deinterleave_two_stage(c, carry, d=d, nb=nb, subp=subp):
            span = DIL_BLOCK * d
            quarter = span // 4
            for src_ref, dst in ((q_ref, None), (k_ref, ks), (v_ref, vs)):
                for r4 in range(4):
                    tmp[r4 * quarter:(r4 + 1) * quarter, :] = (
                        src_ref[pl.ds(c * span + r4, quarter, stride=4), :])
                for r in range(d):
                    x = tmp[pl.ds((r % 4) * quarter + r // 4, DIL_BLOCK, stride=4), :]
                    if dst is None:
                        dq = pl.multiple_of((r * nb + c) * (2 * DIL_BLOCK), 2 * DIL_BLOCK)
                        qs[pl.ds(dq, DIL_BLOCK), :] = jnp.where(is_h0, x, 0.0).astype(BF16)
                        qs[pl.ds(dq + DIL_BLOCK, DIL_BLOCK), :] = (
                            jnp.where(is_h0, 0.0, x).astype(BF16))
                    else:
                        dk = pl.multiple_of(r * subp + DIL_HALF + c * DIL_BLOCK, DIL_HALF)
                        dst[pl.ds(dk, DIL_BLOCK), :] = x.astype(BF16)
            return carry

        if d == 16:
            lax.fori_loop(0, nb, deinterleave_two_stage, 0)
        else:
            lax.fori_loop(0, d * nb, deinterleave, 0)

        def blocks(jj, carry, d=d, nb=nb, subp=subp, first=(bi == 0)):
            js = [jj * DIL_UNROLL + t for t in range(DIL_UNROLL)]
            rn = [(j // nb, j - (j // nb) * nb) for j in js]
            kwins = [pl.ds(pl.multiple_of(r * subp + n * DIL_BLOCK, DIL_BLOCK), 2 * DIL_BLOCK)
                     for r, n in rn]
            scores = []
            for j, (r, n), kwin in zip(js, rn, kwins):
                dq = pl.ds(pl.multiple_of(j * (2 * DIL_BLOCK), 2 * DIL_BLOCK), 2 * DIL_BLOCK)
                variant = jnp.where(n == 0, 1, 0) + jnp.where(n == nb - 1, 2, 0)
                s = lax.dot_general(qs[dq, :], ks[kwin, :], _NT, preferred_element_type=F32)
                scores.append(s + bias_ref[variant])
            stats = []
            for s in scores:
                m = jnp.max(s, axis=-1, keepdims=True)
                stats.append((jnp.exp2(s - m).astype(BF16), m))
            outs = [jnp.dot(p, jnp.concatenate([vs[kwin, :], ones_blk], axis=1),
                            preferred_element_type=F32)
                    for (p, _), kwin in zip(stats, kwins)]
            for (r, n), o2, (_, m) in zip(rn, outs, stats):
                o_b = jnp.where(is_h0, o2[:DIL_BLOCK, :LANES], o2[DIL_BLOCK:, :LANES])
                l_b = jnp.where(is_h0, o2[:DIL_BLOCK, LANES:], o2[DIL_BLOCK:, LANES:])
                m_b = jnp.where(is_h0, m[:DIL_BLOCK], m[DIL_BLOCK:])
                rows = pl.ds(r + n * (DIL_BLOCK * d), DIL_BLOCK, stride=d)
                if first:
                    acc[rows, :] = o_b
                    m_s[rows, :] = m_b
                    l_s[rows, :] = l_b
                else:
                    m_old = m_s[rows, :]
                    m_new = jnp.maximum(m_old, m_b)
                    a = jnp.exp2(m_old - m_new)
                    b = jnp.exp2(m_b - m_new)
                    acc[rows, :] = acc[rows, :] * a + o_b * b
                    l_s[rows, :] = l_s[rows, :] * a + l_b * b
                    m_s[rows, :] = m_new
            return carry

        lax.fori_loop(0, d * nb // DIL_UNROLL, blocks, 0)

    def finish(c, carry):
        rows = pl.ds(pl.multiple_of(c * 256, 256), 256)
        o_ref[rows, :] = (acc[rows, :] / l_s[rows, :]).astype(o_ref.dtype)
        return carry

    lax.fori_loop(0, seq // 256, finish, 0)


def _dil_bias():
    q = np.arange(DIL_BLOCK)[:, None]
    k = np.arange(2 * DIL_BLOCK)[None, :]
    band = np.abs(k - DIL_HALF - q) <= DIL_HALF
    first = k >= DIL_HALF
    last = k < DIL_HALF + DIL_BLOCK
    masks = np.stack([band, band & first, band & last, band & first & last])
    bias = np.where(masks, 0.0, NEG_BIG).astype(np.float32)
    return np.concatenate([bias, bias], axis=1)


def _dilated_attention(qd, kd, vd, seq):
    b = qd.shape[0]
    assert (seq // DIL_BLOCK) % DIL_UNROLL == 0
    max_rows = max(seq + 2 * DIL_HALF * d for _, d in DIL_BRANCHES)
    blk = pl.BlockSpec((None, seq, LANES), lambda i, h: (i, 0, h))
    return pl.pallas_call(
        functools.partial(_dil_kernel, seq=seq),
        grid=(b, D_DIL // LANES),
        in_specs=[blk, blk, blk,
                  pl.BlockSpec((4, 2 * DIL_BLOCK, 2 * DIL_BLOCK), lambda i, h: (0, 0, 0))],
        out_specs=blk,
        out_shape=jax.ShapeDtypeStruct((b, seq, D_DIL), BF16),
        scratch_shapes=[pltpu.VMEM((2 * seq, LANES), BF16),
                        pltpu.VMEM((max_rows, LANES), BF16), pltpu.VMEM((max_rows, LANES), BF16),
                        pltpu.VMEM((seq, LANES), F32), pltpu.VMEM((seq, LANES), F32),
                        pltpu.VMEM((seq, LANES), F32),
                        pltpu.VMEM((DIL_BLOCK * max(d for _, d in DIL_BRANCHES), LANES), F32)],
        compiler_params=pltpu.CompilerParams(dimension_semantics=("parallel", "parallel"),
                                             vmem_limit_bytes=VMEM_LIMIT),
        name="dilated_attention",
    )(qd, kd, vd, jnp.asarray(_dil_bias()))


def _na_kernel(q_ref, k_ref, v_ref, bias_ref, hmask_ref, o_ref, *, seq):
    nb = seq // NA_QBLOCK
    lane = lax.broadcasted_iota(jnp.int32, (NA_QBLOCK, LANES), 1)
    is_h0 = lane < HEAD_DIM
    ones_blk = jnp.ones((NA_KWIN, LANES), BF16)

    def blocks(ii, carry):
        idx = [ii * NA_UNROLL + t for t in range(NA_UNROLL)]
        rows = [pl.ds(pl.multiple_of(i * NA_QBLOCK, NA_QBLOCK), NA_QBLOCK) for i in idx]
        wins = [pl.ds(pl.multiple_of(jnp.clip(i * NA_QBLOCK - 256, 0, seq - NA_KWIN), LANES),
                      NA_KWIN) for i in idx]
        scores = []
        for i, row, win in zip(idx, rows, wins):
            variant = jnp.where(i < 2, i, jnp.where(i >= nb - 2, i - (nb - 5), 2))
            qb = q_ref[row, :]
            q2 = jnp.concatenate([qb * hmask_ref[0:1, :], qb * hmask_ref[1:2, :]], axis=0)
            s = lax.dot_general(q2, k_ref[win, :], _NT, preferred_element_type=F32)
            scores.append(s + bias_ref[variant])
        probs = [jnp.exp2(s - jnp.max(s, axis=-1, keepdims=True)).astype(BF16) for s in scores]
        for row, win, p in zip(rows, wins, probs):
            vw = jnp.concatenate([v_ref[win, :], ones_blk], axis=1)
            o2 = jnp.dot(p, vw, preferred_element_type=F32)
            o2 = o2[:, :LANES] / o2[:, LANES:]
            o_ref[row, :] = jnp.where(is_h0, o2[:NA_QBLOCK], o2[NA_QBLOCK:]).astype(o_ref.dtype)
        return carry

    lax.fori_loop(0, nb // NA_UNROLL, blocks, 0)


def _na_geometry(seq):
    rows = seq // GRID_W
    nb = seq // NA_QBLOCK
    assert rows >= 10 and nb >= 5 and nb % NA_UNROLL == 0
    reps = np.array([0, 1, 2, nb - 2, nb - 1])
    ws = np.clip(reps * NA_QBLOCK - 256, 0, seq - NA_KWIN)
    qtok = reps[:, None] * NA_QBLOCK + np.arange(NA_QBLOCK)[None, :]
    ktok = ws[:, None] + np.arange(NA_KWIN)[None, :]
    qr, qc = qtok // GRID_W, qtok % GRID_W
    kr, kc = ktok // GRID_W, ktok % GRID_W
    row_start = np.clip(qr - NA_ROWS // 2, 0, rows - NA_ROWS)
    col_start = np.clip(qc - NA_COLS // 2, 0, GRID_W - NA_COLS)
    kr_, kc_ = kr[:, None, :], kc[:, None, :]
    valid = ((kr_ >= row_start[:, :, None]) & (kr_ < row_start[:, :, None] + NA_ROWS)
             & (kc_ >= col_start[:, :, None]) & (kc_ < col_start[:, :, None] + NA_COLS))
    ri = np.clip(kr_ - qr[:, :, None] + NA_ROWS - 1, 0, 2 * NA_ROWS - 2)
    return ri[:, ::GRID_W, ::GRID_W], valid


def _na_bias(rpb_l, seq):
    ri_tile, valid = _na_geometry(seq)
    n_h = rpb_l.shape[0]
    edge = GRID_W - NA_COLS
    padded = jnp.pad(rpb_l.astype(F32), ((0, 0), (0, 0), (edge, edge)), mode="edge")
    toep = jnp.stack([padded[:, :, GRID_W - 1 - qc:2 * GRID_W - 1 - qc] for qc in range(GRID_W)],
                     axis=2)
    classes = []
    for v in range(ri_tile.shape[0]):
        qrows = [jnp.concatenate([toep[:, ri_tile[v, a, j]] for j in range(ri_tile.shape[2])],
                                 axis=-1) for a in range(ri_tile.shape[1])]
        classes.append(jnp.concatenate(qrows, axis=-2))
    bias = jnp.where(valid[None], jnp.stack(classes, axis=1) * LOG2E, NEG_BIG)
    bias = bias.reshape(n_h // 2, 2, 5, NA_QBLOCK, NA_KWIN).transpose(0, 2, 1, 3, 4)
    return bias.reshape(n_h // 2, 5, 2 * NA_QBLOCK, NA_KWIN)


def _neighbourhood_attention(qn, kn, vn, bias, seq):
    b = qn.shape[0]
    hmask = jnp.asarray((np.arange(LANES)[None, :] // HEAD_DIM == np.arange(2)[:, None]), BF16)
    blk = pl.BlockSpec((None, seq, LANES), lambda i, h: (i, 0, h))
    return pl.pallas_call(
        functools.partial(_na_kernel, seq=seq),
        grid=(b, D_NA // LANES),
        in_specs=[blk, blk, blk,
                  pl.BlockSpec((None, 5, 2 * NA_QBLOCK, NA_KWIN), lambda i, h: (h, 0, 0, 0)),
                  pl.BlockSpec((2, LANES), lambda i, h: (0, 0))],
        out_specs=blk,
        out_shape=jax.ShapeDtypeStruct((b, seq, D_NA), BF16),
        compiler_params=pltpu.CompilerParams(dimension_semantics=("parallel", "parallel"),
                                             vmem_limit_bytes=VMEM_LIMIT),
        name="neighbourhood_attention",
    )(qn, kn, vn, bias, hmask)


def _mix_ffn_kernel(x_ref, yd_ref, yn_ref, gb_ref, u_ref, up_ref, un_ref, cw_ref, wo_ref,
                    g_mix_ref, g_pre_ref, wg_ref, wu_ref, wd_ref, g_post_ref, o_ref, ubuf,
                    *, tiles_per_seq):
    tm = x_ref.shape[0]
    pos = pl.program_id(0) % tiles_per_seq
    keep_prev = jnp.where(pos == 0, 0.0, 1.0)
    keep_next = jnp.where(pos == tiles_per_seq - 1, 0.0, 1.0)
    ubuf[0:8, :] = up_ref[...] * keep_prev
    ubuf[8:8 + tm, :] = u_ref[...]
    ubuf[8 + tm:16 + tm, :] = un_ref[...] * keep_next
    conv = (cw_ref[0:1, :] * ubuf[7:7 + tm, :] + cw_ref[1:2, :] * ubuf[8:8 + tm, :]
            + cw_ref[2:3, :] * ubuf[9:9 + tm, :])
    y_conv = (gb_ref[...] * conv).astype(BF16)
    cat = jnp.concatenate([yd_ref[...], yn_ref[...], y_conv], axis=-1)
    mixed = jnp.dot(cat, wo_ref[...], preferred_element_type=F32)
    x1 = x_ref[...] + _rms(mixed, g_mix_ref[...])

    hf = _rms(x1, g_pre_ref[...]).astype(BF16)
    f = None
    for lo, hi in FF_CHUNKS:
        gate = jnp.dot(hf, wg_ref[:, lo:hi], preferred_element_type=F32)
        up = jnp.dot(hf, wu_ref[:, lo:hi], preferred_element_type=F32)
        act = (gate * (1.0 / (1.0 + jnp.exp(-gate))) * up).astype(BF16)
        part = jnp.dot(act, wd_ref[lo:hi, :], preferred_element_type=F32)
        f = part if f is None else f + part
    o_ref[...] = x1 + _rms(f, g_post_ref[...])


def _mix_ffn(x2, yd, yn, gb, u, p, seq):
    t = x2.shape[0]
    tm = TOKEN_TILE
    sub = tm // 8
    last = t // 8 - 1
    row = lambda w: pl.BlockSpec((tm, w), lambda i: (i, 0))
    vec = pl.BlockSpec((1, D_MODEL), lambda i: (0, 0))
    resident = lambda shape: pl.BlockSpec(shape, lambda i: (0, 0), pipeline_mode=pl.Buffered(1))
    return pl.pallas_call(
        functools.partial(_mix_ffn_kernel, tiles_per_seq=seq // tm),
        grid=(t // tm,),
        in_specs=[row(D_MODEL), row(D_DIL), row(D_NA), row(D_CONV), row(D_CONV),
                  pl.BlockSpec((8, D_CONV), lambda i: (jnp.maximum(i * sub - 1, 0), 0)),
                  pl.BlockSpec((8, D_CONV), lambda i: (jnp.minimum((i + 1) * sub, last), 0)),
                  pl.BlockSpec((3, D_CONV), lambda i: (0, 0)),
                  resident((D_MODEL, D_MODEL)), vec, vec,
                  resident((D_MODEL, D_FF)), resident((D_MODEL, D_FF)), resident((D_FF, D_MODEL)),
                  vec],
        out_specs=row(D_MODEL),
        out_shape=jax.ShapeDtypeStruct((t, D_MODEL), F32),
        scratch_shapes=[pltpu.VMEM((tm + 16, D_CONV), F32)],
        compiler_params=pltpu.CompilerParams(dimension_semantics=("parallel",),
                                             vmem_limit_bytes=VMEM_LIMIT),
        name="mix_ffn",
    )(x2, yd, yn, gb, u, u, u, p["conv_w"], p["w_out"], p["g_post_mix"], p["g_pre_ffn"],
      p["w_gate"], p["w_up"], p["w_down"], p["g_post_ffn"])


def _rope_tables(seq):
    pos = jnp.arange(seq, dtype=F32)
    inv = 1.0 / (ROPE_THETA ** (jnp.arange(0, HEAD_DIM, 2, dtype=F32) / HEAD_DIM))
    ang = pos[:, None] * inv[None, :]
    reps = LANES // (HEAD_DIM // 2)
    cos_t = jnp.tile(jnp.cos(ang), (1, reps))
    sin_t = jnp.tile(jnp.sin(ang), (1, reps))
    first_half = (np.arange(LANES) % HEAD_DIM) < HEAD_DIM // 2
    sina_t = jnp.where(first_half[None, :], -sin_t, 0.0)
    sinb_t = jnp.where(first_half[None, :], 0.0, sin_t)
    return cos_t, sina_t, sinb_t


def _layer(x, tables, p):
    b, seq, _ = x.shape
    assert seq % 2048 == 0 and seq % TOKEN_TILE == 0
    x2 = x.reshape(b * seq, D_MODEL)
    qd, kd, vd, qn, kn, vn, gb, u = _inproj(x2, p["g_pre_mix"], p["w_in"], *tables, seq)
    to3 = lambda a: a.reshape(b, seq, a.shape[-1])
    yd = _dilated_attention(to3(qd), to3(kd), to3(vd), seq).reshape(b * seq, D_DIL)
    na_bias = _na_bias(p["rpb"], seq)
    yn = _neighbourhood_attention(to3(qn), to3(kn), to3(vn), na_bias, seq).reshape(b * seq, D_NA)
    out = _mix_ffn(x2, yd, yn, gb, u, p, seq)
    return out.reshape(b, seq, D_MODEL)


def kernel(x_prompt, x_sample, g_pre_mix, w_in, conv_w, rpb, w_out, g_post_mix, g_pre_ffn,
           w_gate, w_up, w_down, g_post_ffn):
    depth = w_in.shape[0]
    streams = [x_prompt, x_sample]
    tables = [_rope_tables(x.shape[1]) for x in streams]
    for l in range(depth):
        p = dict(
            g_pre_mix=g_pre_mix[l][None, :], w_in=w_in[l].astype(BF16), conv_w=conv_w[l],
            rpb=rpb[l], w_out=w_out[l].astype(BF16), g_post_mix=g_post_mix[l][None, :],
            g_pre_ffn=g_pre_ffn[l][None, :], w_gate=w_gate[l].astype(BF16),
            w_up=w_up[l].astype(BF16), w_down=w_down[l].astype(BF16),
            g_post_ffn=g_post_ffn[l][None, :])
        streams = [_layer(x, t, p) for x, t in zip(streams, tables)]
    return tuple(streams)
```

```python
import functools

import numpy as np
import jax
import jax.numpy as jnp
from jax import lax
from jax.experimental import pallas as pl
from jax.experimental.pallas import tpu as pltpu

F32 = jnp.float32
BF16 = jnp.bfloat16

D_MODEL = 1024
HEAD_DIM = 64
D_DIL = 384
D_NA = 256
D_CONV = 384
D_IN = 3072
D_FF = 2816
DIL_BRANCHES = ((128, 1), (512, 4), (2048, 16))
DIL_BLOCK = 128
DIL_HALF = 64
DIL_UNROLL = 8
GRID_W = 64
NA_ROWS = 8
NA_COLS = 16
NA_QBLOCK = 64
NA_KWIN = 512
NA_WIN_BACK = 256
NA_UNROLL = 16
ROPE_THETA = 10000.0
RMS_EPS = 1e-6
NEG_BIG = -1e30
LOG2E = 1.4426950408889634
Q_SCALE = HEAD_DIM ** -0.5 * LOG2E

LANES = 128
MXU_WIDTH = 256
TOKEN_TILE = 512
INPROJ_CHUNK = 3 * MXU_WIDTH
FF_CHUNKS = ((0, 5 * MXU_WIDTH), (5 * MXU_WIDTH, D_FF))
VMEM_LIMIT = 56 * 1024 * 1024

_NT = (((1,), (1,)), ((), ()))


def _rms(x, g):
    ms = jnp.mean(x * x, axis=-1, keepdims=True)
    return x * lax.rsqrt(ms + RMS_EPS) * g


def _inproj_kernel(x_ref, g_ref, w_ref, cos_ref, sina_ref, sinb_ref,
                   qd_ref, kd_ref, vd_ref, qn_ref, kn_ref, vn_ref, gb_ref, u_ref):
    hn = _rms(x_ref[...], g_ref[...]).astype(BF16)

    groups = []
    for c in range(0, D_IN, INPROJ_CHUNK):
        r = jnp.dot(hn, w_ref[:, c:c + INPROJ_CHUNK], preferred_element_type=F32)
        groups += [r[:, k * LANES:(k + 1) * LANES] for k in range(INPROJ_CHUNK // LANES)]

    def take(n):
        out = groups[:n]
        del groups[:n]
        return out

    cos = cos_ref[...]
    sina = sina_ref[...]
    sinb = sinb_ref[...]

    def rope(t):
        return t * cos + pltpu.roll(t, 96, 1) * sina + pltpu.roll(t, 32, 1) * sinb

    def put(ref, parts):
        for j, part in enumerate(parts):
            ref[:, j * LANES:(j + 1) * LANES] = part.astype(ref.dtype)

    n_d, n_n, n_c = D_DIL // LANES, D_NA // LANES, D_CONV // LANES
    put(qd_ref, [rope(t) * Q_SCALE for t in take(n_d)])
    put(kd_ref, [rope(t) for t in take(n_d)])
    put(vd_ref, take(n_d))
    put(qn_ref, [t * Q_SCALE for t in take(n_n)])
    put(kn_ref, take(n_n))
    put(vn_ref, take(n_n))
    put(gb_ref, take(n_c))
    gate_c = take(n_c)
    put(u_ref, [c * h for c, h in zip(gate_c, take(n_c))])


def _inproj(x2, g, w_bf, cos_t, sina_t, sinb_t, seq):
    t = x2.shape[0]
    tm = TOKEN_TILE
    tps = seq // tm
    row = lambda w: pl.BlockSpec((tm, w), lambda i: (i, 0))
    tab = pl.BlockSpec((tm, LANES), lambda i: (i % tps, 0))
    out_shape = (
        jax.ShapeDtypeStruct((t, D_DIL), F32), jax.ShapeDtypeStruct((t, D_DIL), F32),
        jax.ShapeDtypeStruct((t, D_DIL), F32),
        jax.ShapeDtypeStruct((t, D_NA), BF16), jax.ShapeDtypeStruct((t, D_NA), BF16),
        jax.ShapeDtypeStruct((t, D_NA), BF16),
        jax.ShapeDtypeStruct((t, D_CONV), F32), jax.ShapeDtypeStruct((t, D_CONV), F32),
    )
    return pl.pallas_call(
        _inproj_kernel,
        grid=(t // tm,),
        in_specs=[row(D_MODEL),
                  pl.BlockSpec((1, D_MODEL), lambda i: (0, 0)),
                  pl.BlockSpec((D_MODEL, D_IN), lambda i: (0, 0), pipeline_mode=pl.Buffered(1)),
                  tab, tab, tab],
        out_specs=(row(D_DIL), row(D_DIL), row(D_DIL), row(D_NA), row(D_NA), row(D_NA),
                   row(D_CONV), row(D_CONV)),
        out_shape=out_shape,
        compiler_params=pltpu.CompilerParams(dimension_semantics=("parallel",),
                                             vmem_limit_bytes=VMEM_LIMIT),
        name="inproj",
    )(x2, g, w_bf, cos_t, sina_t, sinb_t)


def _dil_kernel(q_ref, k_ref, v_ref, bias_ref, o_ref, qs, ks, vs, acc, m_s, l_s, tmp, *, seq):
    lane = lax.broadcasted_iota(jnp.int32, (DIL_BLOCK, LANES), 1)
    is_h0 = lane < HEAD_DIM
    zeros_pad = jnp.zeros((DIL_HALF, LANES), BF16)
    ones_blk = jnp.ones((2 * DIL_BLOCK, LANES), BF16)

    for bi, (_, d) in enumerate(sorted(DIL_BRANCHES, key=lambda wd: -wd[1])):
        sub = seq // d
        nb = sub // DIL_BLOCK
        subp = sub + 2 * DIL_HALF

        for r in range(d):
            for buf in (ks, vs):
                buf[r * subp:r * subp + DIL_HALF, :] = zeros_pad
                buf[r * subp + DIL_HALF + sub:(r + 1) * subp, :] = zeros_pad

        def deinterleave(j, carry, d=d, nb=nb, subp=subp):
            r = j // nb
            c = j - r * nb
            src = pl.ds(r + c * (DIL_BLOCK * d), DIL_BLOCK, stride=d)
            qf = q_ref[src, :]
            dq = pl.multiple_of(j * (2 * DIL_BLOCK), 2 * DIL_BLOCK)
            qs[pl.ds(dq, DIL_BLOCK), :] = jnp.where(is_h0, qf, 0.0).astype(BF16)
            qs[pl.ds(dq + DIL_BLOCK, DIL_BLOCK), :] = jnp.where(is_h0, 0.0, qf).astype(BF16)
            dk = pl.ds(pl.multiple_of(r * subp + DIL_HALF + c * DIL_BLOCK, DIL_HALF), DIL_BLOCK)
            ks[dk, :] = k_ref[src, :].astype(BF16)
            vs[dk, :] = v_ref[src, :].astype(BF16)
            return carry

        def deinterleave_two_stage(c, carry, d=d, nb=nb, subp=subp):
            span = DIL_BLOCK * d
            quarter = span // 4
            for src_ref, dst in ((q_ref, None), (k_ref, ks), (v_ref, vs)):
                for r4 in range(4):
                    tmp[r4 * quarter:(r4 + 1) * quarter, :] = (
                        src_ref[pl.ds(c * span + r4, quarter, stride=4), :])
                for r in range(d):
                    x = tmp[pl.ds((r % 4) * quarter + r // 4, DIL_BLOCK, stride=4), :]
                    if dst is None:
                        dq = pl.multiple_of((r * nb + c) * (2 * DIL_BLOCK), 2 * DIL_BLOCK)
                        qs[pl.ds(dq, DIL_BLOCK), :] = jnp.where(is_h0, x, 0.0).astype(BF16)
                        qs[pl.ds(dq + DIL_BLOCK, DIL_BLOCK), :] = (
                            jnp.where(is_h0, 0.0, x).astype(BF16))
                    else:
                        dk = pl.multiple_of(r * subp + DIL_HALF + c * DIL_BLOCK, DIL_HALF)
                        dst[pl.ds(dk, DIL_BLOCK), :] = x.astype(BF16)
            return carry

        if d == 16:
            lax.fori_loop(0, nb, deinterleave_two_stage, 0)
        else:
            lax.fori_loop(0, d * nb, deinterleave, 0)

        def blocks(jj, carry, d=d, nb=nb, subp=subp, first=(bi == 0)):
            js = [jj * DIL_UNROLL + t for t in range(DIL_UNROLL)]
            rn = [(j // nb, j - (j // nb) * nb) for j in js]
            kwins = [pl.ds(pl.multiple_of(r * subp + n * DIL_BLOCK, DIL_BLOCK), 2 * DIL_BLOCK)
                     for r, n in rn]
            scores = []
            for j, (r, n), kwin in zip(js, rn, kwins):
                dq = pl.ds(pl.multiple_of(j * (2 * DIL_BLOCK), 2 * DIL_BLOCK), 2 * DIL_BLOCK)
                variant = jnp.where(n == 0, 1, 0) + jnp.where(n == nb - 1, 2, 0)
                s = lax.dot_general(qs[dq, :], ks[kwin, :], _NT, preferred_element_type=F32)
                scores.append(s + bias_ref[variant])
            stats = []
            for s in scores:
                m = jnp.max(s, axis=-1, keepdims=True)
                stats.append((jnp.exp2((s - m).astype(BF16)), m))
            outs = [jnp.dot(p, jnp.concatenate([vs[kwin, :], ones_blk], axis=1),
                            preferred_element_type=F32)
                    for (p, _), kwin in zip(stats, kwins)]
            for (r, n), o2, (_, m) in zip(rn, outs, stats):
                o_b = jnp.where(is_h0, o2[:DIL_BLOCK, :LANES], o2[DIL_BLOCK:, :LANES])
                l_b = jnp.where(is_h0, o2[:DIL_BLOCK, LANES:], o2[DIL_BLOCK:, LANES:])
                m_b = jnp.where(is_h0, m[:DIL_BLOCK], m[DIL_BLOCK:])
                rows = pl.ds(r + n * (DIL_BLOCK * d), DIL_BLOCK, stride=d)
                if first:
                    acc[rows, :] = o_b
                    m_s[rows, :] = m_b
                    l_s[rows, :] = l_b
                else:
                    m_old = m_s[rows, :]
                    m_new = jnp.maximum(m_old, m_b)
                    a = jnp.exp2(m_old - m_new)
                    b = jnp.exp2(m_b - m_new)
                    acc[rows, :] = acc[rows, :] * a + o_b * b
                    l_s[rows, :] = l_s[rows, :] * a + l_b * b
                    m_s[rows, :] = m_new
            return carry

        lax.fori_loop(0, d * nb // DIL_UNROLL, blocks, 0)

    def finish(c, carry):
        rows = pl.ds(pl.multiple_of(c * 256, 256), 256)
        o_ref[rows, :] = (acc[rows, :] / l_s[rows, :]).astype(o_ref.dtype)
        return carry

    lax.fori_loop(0, seq // 256, finish, 0)


def _dil_bias():
    q = np.arange(DIL_BLOCK)[:, None]
    k = np.arange(2 * DIL_BLOCK)[None, :]
    band = np.abs(k - DIL_HALF - q) <= DIL_HALF
    first = k >= DIL_HALF
    last = k < DIL_HALF + DIL_BLOCK
    masks = np.stack([band, band & first, band & last, band & first & last])
    bias = np.where(masks, 0.0, NEG_BIG).astype(np.float32)
    return np.concatenate([bias, bias], axis=1)


def _dilated_attention(qd, kd, vd, seq):
    b = qd.shape[0]
    assert (seq // DIL_BLOCK) % DIL_UNROLL == 0
    max_rows = max(seq + 2 * DIL_HALF * d for _, d in DIL_BRANCHES)
    blk = pl.BlockSpec((None, seq, LANES), lambda i, h: (i, 0, h))
    return pl.pallas_call(
        functools.partial(_dil_kernel, seq=seq),
        grid=(b, D_DIL // LANES),
        in_specs=[blk, blk, blk,
                  pl.BlockSpec((4, 2 * DIL_BLOCK, 2 * DIL_BLOCK), lambda i, h: (0, 0, 0))],
        out_specs=blk,
        out_shape=jax.ShapeDtypeStruct((b, seq, D_DIL), BF16),
        scratch_shapes=[pltpu.VMEM((2 * seq, LANES), BF16),
                        pltpu.VMEM((max_rows, LANES), BF16), pltpu.VMEM((max_rows, LANES), BF16),
                        pltpu.VMEM((seq, LANES), F32), pltpu.VMEM((seq, LANES), F32),
                        pltpu.VMEM((seq, LANES), F32),
                        pltpu.VMEM((DIL_BLOCK * max(d for _, d in DIL_BRANCHES), LANES), F32)],
        compiler_params=pltpu.CompilerParams(dimension_semantics=("parallel", "parallel"),
                                             vmem_limit_bytes=VMEM_LIMIT),
        name="dilated_attention",
    )(qd, kd, vd, jnp.asarray(_dil_bias()))


def _na_classes(seq):
    nb = seq // NA_QBLOCK
    n_top = NA_WIN_BACK // NA_QBLOCK
    n_bot = nb - 1 - (seq - NA_KWIN + NA_WIN_BACK) // NA_QBLOCK
    return nb, n_top, n_bot


def _na_kernel(q_ref, k_ref, v_ref, bias_ref, hmask_ref, o_ref, *, seq):
    nb, n_top, n_bot = _na_classes(seq)
    lane = lax.broadcasted_iota(jnp.int32, (NA_QBLOCK, LANES), 1)
    is_h0 = lane < HEAD_DIM
    ones_blk = jnp.ones((NA_KWIN, LANES), BF16)

    def blocks(ii, carry):
        idx = [ii * NA_UNROLL + t for t in range(NA_UNROLL)]
        rows = [pl.ds(pl.multiple_of(i * NA_QBLOCK, NA_QBLOCK), NA_QBLOCK) for i in idx]
        wins = [pl.ds(pl.multiple_of(jnp.clip(i * NA_QBLOCK - NA_WIN_BACK, 0, seq - NA_KWIN),
                                     GRID_W), NA_KWIN) for i in idx]
        scores = []
        for i, row, win in zip(idx, rows, wins):
            variant = jnp.where(i < n_top, i,
                                jnp.where(i >= nb - n_bot, i - (nb - n_bot) + n_top + 1, n_top))
            qb = q_ref[row, :]
            q2 = jnp.concatenate([qb * hmask_ref[0:1, :], qb * hmask_ref[1:2, :]], axis=0)
            s = lax.dot_general(q2, k_ref[win, :], _NT, preferred_element_type=F32)
            scores.append(s + bias_ref[variant])
        probs = [jnp.exp2((s - jnp.max(s, axis=-1, keepdims=True)).astype(BF16)) for s in scores]
        for row, win, p in zip(rows, wins, probs):
            vw = jnp.concatenate([v_ref[win, :], ones_blk], axis=1)
            o2 = jnp.dot(p, vw, preferred_element_type=F32)
            o2 = o2[:, :LANES] / o2[:, LANES:]
            o_ref[row, :] = jnp.where(is_h0, o2[:NA_QBLOCK], o2[NA_QBLOCK:]).astype(o_ref.dtype)
        return carry

    lax.fori_loop(0, nb // NA_UNROLL, blocks, 0)


def _na_geometry(seq):
    rows = seq // GRID_W
    nb, n_top, n_bot = _na_classes(seq)
    assert rows >= 2 * NA_ROWS and nb % NA_UNROLL == 0 and nb > n_top + n_bot
    reps = np.array(list(range(n_top + 1)) + [nb - n_bot + t for t in range(n_bot)])
    ws = np.clip(reps * NA_QBLOCK - NA_WIN_BACK, 0, seq - NA_KWIN)
    qtok = reps[:, None] * NA_QBLOCK + np.arange(NA_QBLOCK)[None, :]
    ktok = ws[:, None] + np.arange(NA_KWIN)[None, :]
    qr, qc = qtok // GRID_W, qtok % GRID_W
    kr, kc = ktok // GRID_W, ktok % GRID_W
    row_start = np.clip(qr - NA_ROWS // 2, 0, rows - NA_ROWS)
    col_start = np.clip(qc - NA_COLS // 2, 0, GRID_W - NA_COLS)
    kr_, kc_ = kr[:, None, :], kc[:, None, :]
    valid = ((kr_ >= row_start[:, :, None]) & (kr_ < row_start[:, :, None] + NA_ROWS)
             & (kc_ >= col_start[:, :, None]) & (kc_ < col_start[:, :, None] + NA_COLS))
    ri = np.clip(kr_ - qr[:, :, None] + NA_ROWS - 1, 0, 2 * NA_ROWS - 2)
    return ri[:, ::GRID_W, ::GRID_W], valid


def _na_bias(rpb_l, seq):
    ri_tile, valid = _na_geometry(seq)
    n_h = rpb_l.shape[0]
    n_cls = ri_tile.shape[0]
    edge = GRID_W - NA_COLS
    padded = jnp.pad(rpb_l.astype(F32), ((0, 0), (0, 0), (edge, edge)), mode="edge")
    toep = jnp.stack([padded[:, :, GRID_W - 1 - qc:2 * GRID_W - 1 - qc] for qc in range(GRID_W)],
                     axis=2)
    classes = []
    for v in range(n_cls):
        qrows = [jnp.concatenate([toep[:, ri_tile[v, a, j]] for j in range(ri_tile.shape[2])],
                                 axis=-1) for a in range(ri_tile.shape[1])]
        classes.append(jnp.concatenate(qrows, axis=-2))
    bias = jnp.where(valid[None], jnp.stack(classes, axis=1) * LOG2E, NEG_BIG)
    bias = bias.reshape(n_h // 2, 2, n_cls, NA_QBLOCK, NA_KWIN).transpose(0, 2, 1, 3, 4)
    return bias.reshape(n_h // 2, n_cls, 2 * NA_QBLOCK, NA_KWIN)


def _neighbourhood_attention(qn, kn, vn, bias, seq):
    b = qn.shape[0]
    hmask = jnp.asarray((np.arange(LANES)[None, :] // HEAD_DIM == np.arange(2)[:, None]), BF16)
    blk = pl.BlockSpec((None, seq, LANES), lambda i, h: (i, 0, h))
    return pl.pallas_call(
        functools.partial(_na_kernel, seq=seq),
        grid=(b, D_NA // LANES),
        in_specs=[blk, blk, blk,
                  pl.BlockSpec((None,) + bias.shape[1:], lambda i, h: (h, 0, 0, 0)),
                  pl.BlockSpec((2, LANES), lambda i, h: (0, 0))],
        out_specs=blk,
        out_shape=jax.ShapeDtypeStruct((b, seq, D_NA), BF16),
        compiler_params=pltpu.CompilerParams(dimension_semantics=("parallel", "parallel"),
                                             vmem_limit_bytes=VMEM_LIMIT),
        name="neighbourhood_attention",
    )(qn, kn, vn, bias, hmask)


def _mix_ffn_kernel(x_ref, yd_ref, yn_ref, gb_ref, u_ref, up_ref, un_ref, cw_ref, wo_ref,
                    g_mix_ref, g_pre_ref, wg_ref, wu_ref, wd_ref, g_post_ref, o_ref, ubuf,
                    *, tiles_per_seq):
    tm = x_ref.shape[0]
    pos = pl.program_id(0) % tiles_per_seq
    keep_prev = jnp.where(pos == 0, 0.0, 1.0)
    keep_next = jnp.where(pos == tiles_per_seq - 1, 0.0, 1.0)
    ubuf[0:8, :] = up_ref[...] * keep_prev
    ubuf[8:8 + tm, :] = u_ref[...]
    ubuf[8 + tm:16 + tm, :] = un_ref[...] * keep_next
    conv = (cw_ref[0:1, :] * ubuf[7:7 + tm, :] + cw_ref[1:2, :] * ubuf[8:8 + tm, :]
            + cw_ref[2:3, :] * ubuf[9:9 + tm, :])
    y_conv = (gb_ref[...] * conv).astype(BF16)
    cat = jnp.concatenate([yd_ref[...], yn_ref[...], y_conv], axis=-1)
    mixed = jnp.dot(cat, wo_ref[...], preferred_element_type=F32)
    x1 = x_ref[...] + _rms(mixed, g_mix_ref[...])

    hf = _rms(x1, g_pre_ref[...]).astype(BF16)
    f = None
    for lo, hi in FF_CHUNKS:
        gate = jnp.dot(hf, wg_ref[:, lo:hi], preferred_element_type=F32)
        up = jnp.dot(hf, wu_ref[:, lo:hi], preferred_element_type=F32)
        act = (gate * (1.0 / (1.0 + jnp.exp(-gate))) * up).astype(BF16)
        part = jnp.dot(act, wd_ref[lo:hi, :], preferred_element_type=F32)
        f = part if f is None else f + part
    o_ref[...] = x1 + _rms(f, g_post_ref[...])


def _mix_ffn(x2, yd, yn, gb, u, p, seq):
    t = x2.shape[0]
    tm = TOKEN_TILE
    sub = tm // 8
    last = t // 8 - 1
    row = lambda w: pl.BlockSpec((tm, w), lambda i: (i, 0))
    vec = pl.BlockSpec((1, D_MODEL), lambda i: (0, 0))
    resident = lambda shape: pl.BlockSpec(shape, lambda i: (0, 0), pipeline_mode=pl.Buffered(1))
    return pl.pallas_call(
        functools.partial(_mix_ffn_kernel, tiles_per_seq=seq // tm),
        grid=(t // tm,),
        in_specs=[row(D_MODEL), row(D_DIL), row(D_NA), row(D_CONV), row(D_CONV),
                  pl.BlockSpec((8, D_CONV), lambda i: (jnp.maximum(i * sub - 1, 0), 0)),
                  pl.BlockSpec((8, D_CONV), lambda i: (jnp.minimum((i + 1) * sub, last), 0)),
                  pl.BlockSpec((3, D_CONV), lambda i: (0, 0)),
                  resident((D_MODEL, D_MODEL)), vec, vec,
                  resident((D_MODEL, D_FF)), resident((D_MODEL, D_FF)), resident((D_FF, D_MODEL)),
                  vec],
        out_specs=row(D_MODEL),
        out_shape=jax.ShapeDtypeStruct((t, D_MODEL), F32),
        scratch_shapes=[pltpu.VMEM((tm + 16, D_CONV), F32)],
        compiler_params=pltpu.CompilerParams(dimension_semantics=("parallel",),
                                             vmem_limit_bytes=VMEM_LIMIT),
        name="mix_ffn",
    )(x2, yd, yn, gb, u, u, u, p["conv_w"], p["w_out"], p["g_post_mix"], p["g_pre_ffn"],
      p["w_gate"], p["w_up"], p["w_down"], p["g_post_ffn"])


def _rope_tables(seq):
    pos = jnp.arange(seq, dtype=F32)
    inv = 1.0 / (ROPE_THETA ** (jnp.arange(0, HEAD_DIM, 2, dtype=F32) / HEAD_DIM))
    ang = pos[:, None] * inv[None, :]
    reps = LANES // (HEAD_DIM // 2)
    cos_t = jnp.tile(jnp.cos(ang), (1, reps))
    sin_t = jnp.tile(jnp.sin(ang), (1, reps))
    first_half = (np.arange(LANES) % HEAD_DIM) < HEAD_DIM // 2
    sina_t = jnp.where(first_half[None, :], -sin_t, 0.0)
    sinb_t = jnp.where(first_half[None, :], 0.0, sin_t)
    return cos_t, sina_t, sinb_t


def _layer(x, tables, p):
    b, seq, _ = x.shape
    assert seq % 2048 == 0 and seq % TOKEN_TILE == 0
    x2 = x.reshape(b * seq, D_MODEL)
    qd, kd, vd, qn, kn, vn, gb, u = _inproj(x2, p["g_pre_mix"], p["w_in"], *tables, seq)
    to3 = lambda a: a.reshape(b, seq, a.shape[-1])
    yd = _dilated_attention(to3(qd), to3(kd), to3(vd), seq).reshape(b * seq, D_DIL)
    na_bias = _na_bias(p["rpb"], seq)
    yn = _neighbourhood_attention(to3(qn), to3(kn), to3(vn), na_bias, seq).reshape(b * seq, D_NA)
    out = _mix_ffn(x2, yd, yn, gb, u, p, seq)
    return out.reshape(b, seq, D_MODEL)


def kernel(x_prompt, x_sample, g_pre_mix, w_in, conv_w, rpb, w_out, g_post_mix, g_pre_ffn,
           w_gate, w_up, w_down, g_post_ffn):
    depth = w_in.shape[0]
    streams = [x_prompt, x_sample]
    tables = [_rope_tables(x.shape[1]) for x in streams]
    for l in range(depth):
        p = dict(
            g_pre_mix=g_pre_mix[l][None, :], w_in=w_in[l].astype(BF16), conv_w=conv_w[l],
            rpb=rpb[l], w_out=w_out[l].astype(BF16), g_post_mix=g_post_mix[l][None, :],
            g_pre_ffn=g_pre_ffn[l][None, :], w_gate=w_gate[l].astype(BF16),
            w_up=w_up[l].astype(BF16), w_down=w_down[l].astype(BF16),
            g_post_ffn=g_post_ffn[l][None, :])
        streams = [_layer(x, t, p) for x, t in zip(streams, tables)]
    return tuple(streams)
```

```python
import functools

import numpy as np
import jax
import jax.numpy as jnp
from jax import lax
from jax.experimental import pallas as pl
from jax.experimental.pallas import tpu as pltpu

F32 = jnp.float32
BF16 = jnp.bfloat16

D_MODEL = 1024
HEAD_DIM = 64
D_DIL = 384
D_NA = 256
D_CONV = 384
D_IN = 3072
D_FF = 2816
DIL_BRANCHES = ((128, 1), (512, 4), (2048, 16))
DIL_BLOCK = 128
DIL_HALF = 64
DIL_UNROLL = 8
GRID_W = 64
NA_ROWS = 8
NA_COLS = 16
NA_QBLOCK = 64
NA_KWIN = 512
NA_WIN_BACK = 256
NA_UNROLL = 16
ROPE_THETA = 10000.0
RMS_EPS = 1e-6
NEG_BIG = -1e30
LOG2E = 1.4426950408889634
Q_SCALE = HEAD_DIM ** -0.5 * LOG2E

LANES = 128
MXU_WIDTH = 256
TOKEN_TILE = 512
ROW_SPLIT = 4
INPROJ_TILE = 1024
INPROJ_CHUNK = 3 * MXU_WIDTH
FF_CHUNKS = ((0, 5 * MXU_WIDTH), (5 * MXU_WIDTH, D_FF))
VMEM_LIMIT = 56 * 1024 * 1024

_NT = (((1,), (1,)), ((), ()))


def _rms(x, g):
    ms = jnp.mean(x * x, axis=-1, keepdims=True)
    return x * lax.rsqrt(ms + RMS_EPS) * g


def _inproj_kernel(x_ref, g_ref, w_ref, cos_ref, sina_ref, sinb_ref,
                   qd_ref, kd_ref, vd_ref, qn_ref, kn_ref, vn_ref, gb_ref, u_ref):
    hn = _rms(x_ref[...], g_ref[...]).astype(BF16)

    groups = []
    for c in range(0, D_IN, INPROJ_CHUNK):
        r = jnp.dot(hn, w_ref[:, c:c + INPROJ_CHUNK], preferred_element_type=F32)
        groups += [r[:, k * LANES:(k + 1) * LANES] for k in range(INPROJ_CHUNK // LANES)]

    def take(n):
        out = groups[:n]
        del groups[:n]
        return out

    cos = cos_ref[...]
    sina = sina_ref[...]
    sinb = sinb_ref[...]

    def rope(t):
        return t * cos + pltpu.roll(t, 96, 1) * sina + pltpu.roll(t, 32, 1) * sinb

    def put(ref, parts):
        for j, part in enumerate(parts):
            ref[:, j * LANES:(j + 1) * LANES] = part.astype(ref.dtype)

    n_d, n_n, n_c = D_DIL // LANES, D_NA // LANES, D_CONV // LANES
    put(qd_ref, [rope(t) * Q_SCALE for t in take(n_d)])
    put(kd_ref, [rope(t) for t in take(n_d)])
    put(vd_ref, take(n_d))
    put(qn_ref, [t * Q_SCALE for t in take(n_n)])
    put(kn_ref, take(n_n))
    put(vn_ref, take(n_n))
    put(gb_ref, take(n_c))
    gate_c = take(n_c)
    put(u_ref, [c * h for c, h in zip(gate_c, take(n_c))])


def _inproj(x2, g, w_bf, cos_t, sina_t, sinb_t, seq):
    t = x2.shape[0]
    tm = INPROJ_TILE
    tps = seq // tm
    row = lambda w: pl.BlockSpec((tm, w), lambda i: (i, 0))
    tab = pl.BlockSpec((tm, LANES), lambda i: (i % tps, 0))
    out_shape = (
        jax.ShapeDtypeStruct((t, D_DIL), F32), jax.ShapeDtypeStruct((t, D_DIL), F32),
        jax.ShapeDtypeStruct((t, D_DIL), F32),
        jax.ShapeDtypeStruct((t, D_NA), BF16), jax.ShapeDtypeStruct((t, D_NA), BF16),
        jax.ShapeDtypeStruct((t, D_NA), BF16),
        jax.ShapeDtypeStruct((t, D_CONV), F32), jax.ShapeDtypeStruct((t, D_CONV), F32),
    )
    return pl.pallas_call(
        _inproj_kernel,
        grid=(t // tm,),
        in_specs=[row(D_MODEL),
                  pl.BlockSpec((1, D_MODEL), lambda i: (0, 0)),
                  pl.BlockSpec((D_MODEL, D_IN), lambda i: (0, 0), pipeline_mode=pl.Buffered(1)),
                  tab, tab, tab],
        out_specs=(row(D_DIL), row(D_DIL), row(D_DIL), row(D_NA), row(D_NA), row(D_NA),
                   row(D_CONV), row(D_CONV)),
        out_shape=out_shape,
        compiler_params=pltpu.CompilerParams(dimension_semantics=("parallel",),
                                             vmem_limit_bytes=VMEM_LIMIT),
        name="inproj",
    )(x2, g, w_bf, cos_t, sina_t, sinb_t)


def _dil_kernel(q_ref, k_ref, v_ref, bias_ref, o_ref, qs, ks, vs, acc, m_s, l_s, tmp, *, seq):
    lane = lax.broadcasted_iota(jnp.int32, (DIL_BLOCK, LANES), 1)
    is_h0 = lane < HEAD_DIM
    zeros_pad = jnp.zeros((DIL_HALF, LANES), BF16)
    ones_blk = jnp.ones((2 * DIL_BLOCK, LANES), BF16)

    for bi, (_, d) in enumerate(sorted(DIL_BRANCHES, key=lambda wd: -wd[1])):
        sub = seq // d
        nb = sub // DIL_BLOCK
        subp = sub + 2 * DIL_HALF

        for r in range(d):
            for buf in (ks, vs):
                buf[r * subp:r * subp + DIL_HALF, :] = zeros_pad
                buf[r * subp + DIL_HALF + sub:(r + 1) * subp, :] = zeros_pad

        def deinterleave(j, carry, d=d, nb=nb, subp=subp):
            r = j // nb
            c = j - r * nb
            src = pl.ds(r + c * (DIL_BLOCK * d), DIL_BLOCK, stride=d)
            qf = q_ref[src, :]
            dq = pl.multiple_of(j * (2 * DIL_BLOCK), 2 * DIL_BLOCK)
            qs[pl.ds(dq, DIL_BLOCK), :] = jnp.where(is_h0, qf, 0.0).astype(BF16)
            qs[pl.ds(dq + DIL_BLOCK, DIL_BLOCK), :] = jnp.where(is_h0, 0.0, qf).astype(BF16)
            dk = pl.ds(pl.multiple_of(r * subp + DIL_HALF + c * DIL_BLOCK, DIL_HALF), DIL_BLOCK)
            ks[dk, :] = k_ref[src, :].astype(BF16)
            vs[dk, :] = v_ref[src, :].astype(BF16)
            return carry

        def deinterleave_two_stage(c, carry, d=d, nb=nb, subp=subp):
            span = DIL_BLOCK * d
            quarter = span // 4
            for src_ref, dst in ((q_ref, None), (k_ref, ks), (v_ref, vs)):
                for r4 in range(4):
                    tmp[r4 * quarter:(r4 + 1) * quarter, :] = (
                        src_ref[pl.ds(c * span + r4, quarter, stride=4), :])
                for r in range(d):
                    x = tmp[pl.ds((r % 4) * quarter + r // 4, DIL_BLOCK, stride=4), :]
                    if dst is None:
                        dq = pl.multiple_of((r * nb + c) * (2 * DIL_BLOCK), 2 * DIL_BLOCK)
                        qs[pl.ds(dq, DIL_BLOCK), :] = jnp.where(is_h0, x, 0.0).astype(BF16)
                        qs[pl.ds(dq + DIL_BLOCK, DIL_BLOCK), :] = (
                            jnp.where(is_h0, 0.0, x).astype(BF16))
                    else:
                        dk = pl.multiple_of(r * subp + DIL_HALF + c * DIL_BLOCK, DIL_HALF)
                        dst[pl.ds(dk, DIL_BLOCK), :] = x.astype(BF16)
            return carry

        if d == 16:
            lax.fori_loop(0, nb, deinterleave_two_stage, 0)
        else:
            lax.fori_loop(0, d * nb, deinterleave, 0)

        def blocks(jj, carry, d=d, nb=nb, subp=subp, first=(bi == 0)):
            js = [jj * DIL_UNROLL + t for t in range(DIL_UNROLL)]
            rn = [(j // nb, j - (j // nb) * nb) for j in js]
            kwins = [pl.ds(pl.multiple_of(r * subp + n * DIL_BLOCK, DIL_BLOCK), 2 * DIL_BLOCK)
                     for r, n in rn]
            scores = []
            for j, (r, n), kwin in zip(js, rn, kwins):
                dq = pl.ds(pl.multiple_of(j * (2 * DIL_BLOCK), 2 * DIL_BLOCK), 2 * DIL_BLOCK)
                variant = jnp.where(n == 0, 1, 0) + jnp.where(n == nb - 1, 2, 0)
                s = lax.dot_general(qs[dq, :], ks[kwin, :], _NT, preferred_element_type=F32)
                scores.append(s + bias_ref[variant])
            stats = []
            for s in scores:
                m = jnp.max(s, axis=-1, keepdims=True)
                stats.append((jnp.exp2((s - m).astype(BF16)), m))
            outs = [jnp.dot(p, jnp.concatenate([vs[kwin, :], ones_blk], axis=1),
                            preferred_element_type=F32)
                    for (p, _), kwin in zip(stats, kwins)]
            for (r, n), o2, (_, m) in zip(rn, outs, stats):
                o_b = jnp.where(is_h0, o2[:DIL_BLOCK, :LANES], o2[DIL_BLOCK:, :LANES])
                l_b = jnp.where(is_h0, o2[:DIL_BLOCK, LANES:], o2[DIL_BLOCK:, LANES:])
                m_b = jnp.where(is_h0, m[:DIL_BLOCK], m[DIL_BLOCK:])
                rows = pl.ds(r + n * (DIL_BLOCK * d), DIL_BLOCK, stride=d)
                if first:
                    acc[rows, :] = o_b
                    m_s[rows, :] = m_b
                    l_s[rows, :] = l_b
                else:
                    m_old = m_s[rows, :]
                    m_new = jnp.maximum(m_old, m_b)
                    a = jnp.exp2(m_old - m_new)
                    b = jnp.exp2(m_b - m_new)
                    acc[rows, :] = acc[rows, :] * a + o_b * b
                    l_s[rows, :] = l_s[rows, :] * a + l_b * b
                    m_s[rows, :] = m_new
            return carry

        lax.fori_loop(0, d * nb // DIL_UNROLL, blocks, 0)

    def finish(c, carry):
        rows = pl.ds(pl.multiple_of(c * 256, 256), 256)
        o_ref[rows, :] = (acc[rows, :] / l_s[rows, :]).astype(o_ref.dtype)
        return carry

    lax.fori_loop(0, seq // 256, finish, 0)


def _dil_bias():
    q = np.arange(DIL_BLOCK)[:, None]
    k = np.arange(2 * DIL_BLOCK)[None, :]
    band = np.abs(k - DIL_HALF - q) <= DIL_HALF
    first = k >= DIL_HALF
    last = k < DIL_HALF + DIL_BLOCK
    masks = np.stack([band, band & first, band & last, band & first & last])
    bias = np.where(masks, 0.0, NEG_BIG).astype(np.float32)
    return np.concatenate([bias, bias], axis=1)


def _dilated_attention(qd, kd, vd, seq):
    b = qd.shape[0]
    assert (seq // DIL_BLOCK) % DIL_UNROLL == 0
    max_rows = max(seq + 2 * DIL_HALF * d for _, d in DIL_BRANCHES)
    blk = pl.BlockSpec((None, seq, LANES), lambda i, h: (i, 0, h))
    return pl.pallas_call(
        functools.partial(_dil_kernel, seq=seq),
        grid=(b, D_DIL // LANES),
        in_specs=[blk, blk, blk,
                  pl.BlockSpec((4, 2 * DIL_BLOCK, 2 * DIL_BLOCK), lambda i, h: (0, 0, 0))],
        out_specs=blk,
        out_shape=jax.ShapeDtypeStruct((b, seq, D_DIL), BF16),
        scratch_shapes=[pltpu.VMEM((2 * seq, LANES), BF16),
                        pltpu.VMEM((max_rows, LANES), BF16), pltpu.VMEM((max_rows, LANES), BF16),
                        pltpu.VMEM((seq, LANES), F32), pltpu.VMEM((seq, LANES), F32),
                        pltpu.VMEM((seq, LANES), F32),
                        pltpu.VMEM((DIL_BLOCK * max(d for _, d in DIL_BRANCHES), LANES), F32)],
        compiler_params=pltpu.CompilerParams(dimension_semantics=("parallel", "parallel"),
                                             vmem_limit_bytes=VMEM_LIMIT),
        name="dilated_attention",
    )(qd, kd, vd, jnp.asarray(_dil_bias()))


def _na_classes(seq):
    nb = seq // NA_QBLOCK
    n_top = NA_WIN_BACK // NA_QBLOCK
    n_bot = nb - 1 - (seq - NA_KWIN + NA_WIN_BACK) // NA_QBLOCK
    return nb, n_top, n_bot


def _na_kernel(q_ref, k_ref, v_ref, bias_ref, hmask_ref, o_ref, *, seq):
    nb, n_top, n_bot = _na_classes(seq)
    lane = lax.broadcasted_iota(jnp.int32, (NA_QBLOCK, LANES), 1)
    is_h0 = lane < HEAD_DIM
    ones_blk = jnp.ones((NA_KWIN, LANES), BF16)

    def blocks(ii, carry):
        idx = [ii * NA_UNROLL + t for t in range(NA_UNROLL)]
        rows = [pl.ds(pl.multiple_of(i * NA_QBLOCK, NA_QBLOCK), NA_QBLOCK) for i in idx]
        wins = [pl.ds(pl.multiple_of(jnp.clip(i * NA_QBLOCK - NA_WIN_BACK, 0, seq - NA_KWIN),
                                     GRID_W), NA_KWIN) for i in idx]
        scores = []
        for i, row, win in zip(idx, rows, wins):
            variant = jnp.where(i < n_top, i,
                                jnp.where(i >= nb - n_bot, i - (nb - n_bot) + n_top + 1, n_top))
            qb = q_ref[row, :]
            q2 = jnp.concatenate([qb * hmask_ref[0:1, :], qb * hmask_ref[1:2, :]], axis=0)
            s = lax.dot_general(q2, k_ref[win, :], _NT, preferred_element_type=F32)
            scores.append(s + bias_ref[variant])
        probs = [jnp.exp2((s - jnp.max(s, axis=-1, keepdims=True)).astype(BF16)) for s in scores]
        for row, win, p in zip(rows, wins, probs):
            vw = jnp.concatenate([v_ref[win, :], ones_blk], axis=1)
            o2 = jnp.dot(p, vw, preferred_element_type=F32)
            o2 = o2[:, :LANES] / o2[:, LANES:]
            o_ref[row, :] = jnp.where(is_h0, o2[:NA_QBLOCK], o2[NA_QBLOCK:]).astype(o_ref.dtype)
        return carry

    lax.fori_loop(0, nb // NA_UNROLL, blocks, 0)


def _na_geometry(seq):
    rows = seq // GRID_W
    nb, n_top, n_bot = _na_classes(seq)
    assert rows >= 2 * NA_ROWS and nb % NA_UNROLL == 0 and nb > n_top + n_bot
    reps = np.array(list(range(n_top + 1)) + [nb - n_bot + t for t in range(n_bot)])
    ws = np.clip(reps * NA_QBLOCK - NA_WIN_BACK, 0, seq - NA_KWIN)
    qtok = reps[:, None] * NA_QBLOCK + np.arange(NA_QBLOCK)[None, :]
    ktok = ws[:, None] + np.arange(NA_KWIN)[None, :]
    qr, qc = qtok // GRID_W, qtok % GRID_W
    kr, kc = ktok // GRID_W, ktok % GRID_W
    row_start = np.clip(qr - NA_ROWS // 2, 0, rows - NA_ROWS)
    col_start = np.clip(qc - NA_COLS // 2, 0, GRID_W - NA_COLS)
    kr_, kc_ = kr[:, None, :], kc[:, None, :]
    valid = ((kr_ >= row_start[:, :, None]) & (kr_ < row_start[:, :, None] + NA_ROWS)
             & (kc_ >= col_start[:, :, None]) & (kc_ < col_start[:, :, None] + NA_COLS))
    ri = np.clip(kr_ - qr[:, :, None] + NA_ROWS - 1, 0, 2 * NA_ROWS - 2)
    return ri[:, ::GRID_W, ::GRID_W], valid


def _na_bias(rpb_l, seq):
    ri_tile, valid = _na_geometry(seq)
    n_h = rpb_l.shape[0]
    n_cls = ri_tile.shape[0]
    edge = GRID_W - NA_COLS
    padded = jnp.pad(rpb_l.astype(F32), ((0, 0), (0, 0), (edge, edge)), mode="edge")
    toep = jnp.stack([padded[:, :, GRID_W - 1 - qc:2 * GRID_W - 1 - qc] for qc in range(GRID_W)],
                     axis=2)
    classes = []
    for v in range(n_cls):
        qrows = [jnp.concatenate([toep[:, ri_tile[v, a, j]] for j in range(ri_tile.shape[2])],
                                 axis=-1) for a in range(ri_tile.shape[1])]
        classes.append(jnp.concatenate(qrows, axis=-2))
    bias = jnp.where(valid[None], jnp.stack(classes, axis=1) * LOG2E, NEG_BIG)
    bias = bias.reshape(n_h // 2, 2, n_cls, NA_QBLOCK, NA_KWIN).transpose(0, 2, 1, 3, 4)
    return bias.reshape(n_h // 2, n_cls, 2 * NA_QBLOCK, NA_KWIN)


def _neighbourhood_attention(qn, kn, vn, bias, seq):
    b = qn.shape[0]
    hmask = jnp.asarray((np.arange(LANES)[None, :] // HEAD_DIM == np.arange(2)[:, None]), BF16)
    blk = pl.BlockSpec((None, seq, LANES), lambda i, h: (i, 0, h))
    return pl.pallas_call(
        functools.partial(_na_kernel, seq=seq),
        grid=(b, D_NA // LANES),
        in_specs=[blk, blk, blk,
                  pl.BlockSpec((None,) + bias.shape[1:], lambda i, h: (h, 0, 0, 0)),
                  pl.BlockSpec((2, LANES), lambda i, h: (0, 0))],
        out_specs=blk,
        out_shape=jax.ShapeDtypeStruct((b, seq, D_NA), BF16),
        compiler_params=pltpu.CompilerParams(dimension_semantics=("parallel", "parallel"),
                                             vmem_limit_bytes=VMEM_LIMIT),
        name="neighbourhood_attention",
    )(qn, kn, vn, bias, hmask)


def _mix_ffn_kernel(x_ref, yd_ref, yn_ref, gb_ref, u_ref, up_ref, un_ref, cw_ref, wo_ref,
                    g_mix_ref, g_pre_ref, wg_ref, wu_ref, wd_ref, g_post_ref, o_ref, ubuf,
                    *, tiles_per_seq):
    tm = x_ref.shape[0]
    pos = pl.program_id(0) % tiles_per_seq
    keep_prev = jnp.where(pos == 0, 0.0, 1.0)
    keep_next = jnp.where(pos == tiles_per_seq - 1, 0.0, 1.0)
    ubuf[0:8, :] = up_ref[...] * keep_prev
    ubuf[8:8 + tm, :] = u_ref[...]
    ubuf[8 + tm:16 + tm, :] = un_ref[...] * keep_next
    hs = tm // ROW_SPLIT
    groups = [slice(h * hs, (h + 1) * hs) for h in range(ROW_SPLIT)]

    def dots(lhs, w):
        return [jnp.dot(a, w, preferred_element_type=F32) for a in lhs]

    cats = []
    for sl in groups:
        lo = 8 + sl.start
        conv = (cw_ref[0:1, :] * ubuf[lo - 1:lo - 1 + hs, :] + cw_ref[1:2, :] * ubuf[lo:lo + hs, :]
                + cw_ref[2:3, :] * ubuf[lo + 1:lo + 1 + hs, :])
        y_conv = (gb_ref[sl, :] * conv).astype(BF16)
        cats.append(jnp.concatenate([yd_ref[sl, :], yn_ref[sl, :], y_conv], axis=-1))
    mixed = dots(cats, wo_ref[...])
    x1 = [x_ref[sl, :] + _rms(m, g_mix_ref[...]) for sl, m in zip(groups, mixed)]
    hf = [_rms(v, g_pre_ref[...]).astype(BF16) for v in x1]
    f = [None] * ROW_SPLIT
    for lo, hi in FF_CHUNKS:
        gate = dots(hf, wg_ref[:, lo:hi])
        up = dots(hf, wu_ref[:, lo:hi])
        act = [(g * (1.0 / (1.0 + jnp.exp(-g))) * v).astype(BF16) for g, v in zip(gate, up)]
        part = dots(act, wd_ref[lo:hi, :])
        f = [p if a is None else a + p for a, p in zip(f, part)]
    for sl, v, a in zip(groups, x1, f):
        o_ref[sl, :] = v + _rms(a, g_post_ref[...])


def _mix_ffn(x2, yd, yn, gb, u, p, seq):
    t = x2.shape[0]
    tm = TOKEN_TILE
    sub = tm // 8
    last = t // 8 - 1
    row = lambda w: pl.BlockSpec((tm, w), lambda i: (i, 0))
    vec = pl.BlockSpec((1, D_MODEL), lambda i: (0, 0))
    resident = lambda shape: pl.BlockSpec(shape, lambda i: (0, 0), pipeline_mode=pl.Buffered(1))
    return pl.pallas_call(
        functools.partial(_mix_ffn_kernel, tiles_per_seq=seq // tm),
        grid=(t // tm,),
        in_specs=[row(D_MODEL), row(D_DIL), row(D_NA), row(D_CONV), row(D_CONV),
                  pl.BlockSpec((8, D_CONV), lambda i: (jnp.maximum(i * sub - 1, 0), 0)),
                  pl.BlockSpec((8, D_CONV), lambda i: (jnp.minimum((i + 1) * sub, last), 0)),
                  pl.BlockSpec((3, D_CONV), lambda i: (0, 0)),
                  resident((D_MODEL, D_MODEL)), vec, vec,
                  resident((D_MODEL, D_FF)), resident((D_MODEL, D_FF)), resident((D_FF, D_MODEL)),
                  vec],
        out_specs=row(D_MODEL),
        out_shape=jax.ShapeDtypeStruct((t, D_MODEL), F32),
        scratch_shapes=[pltpu.VMEM((tm + 16, D_CONV), F32)],
        compiler_params=pltpu.CompilerParams(dimension_semantics=("parallel",),
                                             vmem_limit_bytes=VMEM_LIMIT),
        name="mix_ffn",
    )(x2, yd, yn, gb, u, u, u, p["conv_w"], p["w_out"], p["g_post_mix"], p["g_pre_ffn"],
      p["w_gate"], p["w_up"], p["w_down"], p["g_post_ffn"])


def _rope_tables(seq):
    pos = jnp.arange(seq, dtype=F32)
    inv = 1.0 / (ROPE_THETA ** (jnp.arange(0, HEAD_DIM, 2, dtype=F32) / HEAD_DIM))
    ang = pos[:, None] * inv[None, :]
    reps = LANES // (HEAD_DIM // 2)
    cos_t = jnp.tile(jnp.cos(ang), (1, reps))
    sin_t = jnp.tile(jnp.sin(ang), (1, reps))
    first_half = (np.arange(LANES) % HEAD_DIM) < HEAD_DIM // 2
    sina_t = jnp.where(first_half[None, :], -sin_t, 0.0)
    sinb_t = jnp.where(first_half[None, :], 0.0, sin_t)
    return cos_t, sina_t, sinb_t


def _layer(x, tables, p):
    b, seq, _ = x.shape
    assert seq % 2048 == 0 and seq % TOKEN_TILE == 0 and seq % INPROJ_TILE == 0
    x2 = x.reshape(b * seq, D_MODEL)
    qd, kd, vd, qn, kn, vn, gb, u = _inproj(x2, p["g_pre_mix"], p["w_in"], *tables, seq)
    to3 = lambda a: a.reshape(b, seq, a.shape[-1])
    yd = _dilated_attention(to3(qd), to3(kd), to3(vd), seq).reshape(b * seq, D_DIL)
    na_bias = _na_bias(p["rpb"], seq)
    yn = _neighbourhood_attention(to3(qn), to3(kn), to3(vn), na_bias, seq).reshape(b * seq, D_NA)
    out = _mix_ffn(x2, yd, yn, gb, u, p, seq)
    return out.reshape(b, seq, D_MODEL)


def kernel(x_prompt, x_sample, g_pre_mix, w_in, conv_w, rpb, w_out, g_post_mix, g_pre_ffn,
           w_gate, w_up, w_down, g_post_ffn):
    depth = w_in.shape[0]
    streams = [x_prompt, x_sample]
    tables = [_rope_tables(x.shape[1]) for x in streams]
    for l in range(depth):
        p = dict(
            g_pre_mix=g_pre_mix[l][None, :], w_in=w_in[l].astype(BF16), conv_w=conv_w[l],
            rpb=rpb[l], w_out=w_out[l].astype(BF16), g_post_mix=g_post_mix[l][None, :],
            g_pre_ffn=g_pre_ffn[l][None, :], w_gate=w_gate[l].astype(BF16),
            w_up=w_up[l].astype(BF16), w_down=w_down[l].astype(BF16),
            g_post_ffn=g_post_ffn[l][None, :])
        streams = [_layer(x, t, p) for x, t in zip(streams, tables)]
    return tuple(streams)
```

```python
import functools

import numpy as np
import jax
import jax.numpy as jnp
from jax import lax
from jax.experimental import pallas as pl
from jax.experimental.pallas import tpu as pltpu

F32 = jnp.float32
BF16 = jnp.bfloat16

D_MODEL = 1024
HEAD_DIM = 64
D_DIL = 384
D_NA = 256
D_CONV = 384
D_IN = 3072
D_FF = 2816
DIL_BRANCHES = ((128, 1), (512, 4), (2048, 16))
DIL_BLOCK = 128
DIL_HALF = 64
DIL_UNROLL = 8
GRID_W = 64
NA_ROWS = 8
NA_COLS = 16
NA_QBLOCK = 64
NA_KWIN = 512
NA_WIN_BACK = 256
NA_UNROLL = 32
ROPE_THETA = 10000.0
RMS_EPS = 1e-6
NEG_BIG = -1e30
LOG2E = 1.4426950408889634
Q_SCALE = HEAD_DIM ** -0.5 * LOG2E

LANES = 128
MXU_WIDTH = 256
TOKEN_TILE = 512
ROW_SPLIT = 4
INPROJ_TILE = 1024
INPROJ_CHUNK = 3 * MXU_WIDTH
FF_CHUNKS = ((0, 5 * MXU_WIDTH), (5 * MXU_WIDTH, D_FF))
VMEM_LIMIT = 56 * 1024 * 1024

_NT = (((1,), (1,)), ((), ()))


def _rms(x, g):
    ms = jnp.mean(x * x, axis=-1, keepdims=True)
    return x * lax.rsqrt(ms + RMS_EPS) * g


def _inproj_kernel(x_ref, g_ref, w_ref, cos_ref, sina_ref, sinb_ref,
                   qd_ref, kd_ref, vd_ref, qn_ref, kn_ref, vn_ref, gb_ref, u_ref):
    hn = _rms(x_ref[...], g_ref[...]).astype(BF16)

    groups = []
    for c in range(0, D_IN, INPROJ_CHUNK):
        r = jnp.dot(hn, w_ref[:, c:c + INPROJ_CHUNK], preferred_element_type=F32)
        groups += [r[:, k * LANES:(k + 1) * LANES] for k in range(INPROJ_CHUNK // LANES)]

    def take(n):
        out = groups[:n]
        del groups[:n]
        return out

    cos = cos_ref[...]
    sina = sina_ref[...]
    sinb = sinb_ref[...]

    def rope(t):
        return t * cos + pltpu.roll(t, 96, 1) * sina + pltpu.roll(t, 32, 1) * sinb

    def put(ref, parts):
        for j, part in enumerate(parts):
            ref[:, j * LANES:(j + 1) * LANES] = part.astype(ref.dtype)

    n_d, n_n, n_c = D_DIL // LANES, D_NA // LANES, D_CONV // LANES
    put(qd_ref, [rope(t) * Q_SCALE for t in take(n_d)])
    put(kd_ref, [rope(t) for t in take(n_d)])
    put(vd_ref, take(n_d))
    put(qn_ref, [t * Q_SCALE for t in take(n_n)])
    put(kn_ref, take(n_n))
    put(vn_ref, take(n_n))
    put(gb_ref, take(n_c))
    gate_c = take(n_c)
    put(u_ref, [c * h for c, h in zip(gate_c, take(n_c))])


def _inproj(x2, g, w_bf, cos_t, sina_t, sinb_t, seq):
    t = x2.shape[0]
    tm = INPROJ_TILE
    tps = seq // tm
    row = lambda w: pl.BlockSpec((tm, w), lambda i: (i, 0))
    tab = pl.BlockSpec((tm, LANES), lambda i: (i % tps, 0))
    out_shape = (
        jax.ShapeDtypeStruct((t, D_DIL), F32), jax.ShapeDtypeStruct((t, D_DIL), F32),
        jax.ShapeDtypeStruct((t, D_DIL), F32),
        jax.ShapeDtypeStruct((t, D_NA), BF16), jax.ShapeDtypeStruct((t, D_NA), BF16),
        jax.ShapeDtypeStruct((t, D_NA), BF16),
        jax.ShapeDtypeStruct((t, D_CONV), F32), jax.ShapeDtypeStruct((t, D_CONV), F32),
    )
    return pl.pallas_call(
        _inproj_kernel,
        grid=(t // tm,),
        in_specs=[row(D_MODEL),
                  pl.BlockSpec((1, D_MODEL), lambda i: (0, 0)),
                  pl.BlockSpec((D_MODEL, D_IN), lambda i: (0, 0), pipeline_mode=pl.Buffered(1)),
                  tab, tab, tab],
        out_specs=(row(D_DIL), row(D_DIL), row(D_DIL), row(D_NA), row(D_NA), row(D_NA),
                   row(D_CONV), row(D_CONV)),
        out_shape=out_shape,
        compiler_params=pltpu.CompilerParams(dimension_semantics=("parallel",),
                                             vmem_limit_bytes=VMEM_LIMIT),
        name="inproj",
    )(x2, g, w_bf, cos_t, sina_t, sinb_t)


def _dil_kernel(q_ref, k_ref, v_ref, bias_ref, o_ref, qs, ks, vs, acc, m_s, l_s, tmp, *, seq):
    lane = lax.broadcasted_iota(jnp.int32, (DIL_BLOCK, LANES), 1)
    is_h0 = lane < HEAD_DIM
    zeros_pad = jnp.zeros((DIL_HALF, LANES), BF16)
    ones_blk = jnp.ones((2 * DIL_BLOCK, LANES), BF16)

    for bi, (_, d) in enumerate(sorted(DIL_BRANCHES, key=lambda wd: -wd[1])):
        sub = seq // d
        nb = sub // DIL_BLOCK
        subp = sub + 2 * DIL_HALF

        for r in range(d):
            for buf in (ks, vs):
                buf[r * subp:r * subp + DIL_HALF, :] = zeros_pad
                buf[r * subp + DIL_HALF + sub:(r + 1) * subp, :] = zeros_pad

        def deinterleave(j, carry, d=d, nb=nb, subp=subp):
            r = j // nb
            c = j - r * nb
            src = pl.ds(r + c * (DIL_BLOCK * d), DIL_BLOCK, stride=d)
            qf = q_ref[src, :]
            dq = pl.multiple_of(j * (2 * DIL_BLOCK), 2 * DIL_BLOCK)
            qs[pl.ds(dq, DIL_BLOCK), :] = jnp.where(is_h0, qf, 0.0).astype(BF16)
            qs[pl.ds(dq + DIL_BLOCK, DIL_BLOCK), :] = jnp.where(is_h0, 0.0, qf).astype(BF16)
            dk = pl.ds(pl.multiple_of(r * subp + DIL_HALF + c * DIL_BLOCK, DIL_HALF), DIL_BLOCK)
            ks[dk, :] = k_ref[src, :].astype(BF16)
            vs[dk, :] = v_ref[src, :].astype(BF16)
            return carry

        def deinterleave_two_stage(c, carry, d=d, nb=nb, subp=subp):
            span = DIL_BLOCK * d
            quarter = span // 4
            for src_ref, dst in ((q_ref, None), (k_ref, ks), (v_ref, vs)):
                for r4 in range(4):
                    tmp[r4 * quarter:(r4 + 1) * quarter, :] = (
                        src_ref[pl.ds(c * span + r4, quarter, stride=4), :])
                for r in range(d):
                    x = tmp[pl.ds((r % 4) * quarter + r // 4, DIL_BLOCK, stride=4), :]
                    if dst is None:
                        dq = pl.multiple_of((r * nb + c) * (2 * DIL_BLOCK), 2 * DIL_BLOCK)
                        qs[pl.ds(dq, DIL_BLOCK), :] = jnp.where(is_h0, x, 0.0).astype(BF16)
                        qs[pl.ds(dq + DIL_BLOCK, DIL_BLOCK), :] = (
                            jnp.where(is_h0, 0.0, x).astype(BF16))
                    else:
                        dk = pl.multiple_of(r * subp + DIL_HALF + c * DIL_BLOCK, DIL_HALF)
                        dst[pl.ds(dk, DIL_BLOCK), :] = x.astype(BF16)
            return carry

        if d == 16:
            lax.fori_loop(0, nb, deinterleave_two_stage, 0)
        else:
            lax.fori_loop(0, d * nb, deinterleave, 0)

        def blocks(jj, carry, d=d, nb=nb, subp=subp, first=(bi == 0)):
            js = [jj * DIL_UNROLL + t for t in range(DIL_UNROLL)]
            rn = [(j // nb, j - (j // nb) * nb) for j in js]
            kwins = [pl.ds(pl.multiple_of(r * subp + n * DIL_BLOCK, DIL_BLOCK), 2 * DIL_BLOCK)
                     for r, n in rn]
            scores = []
            for j, (r, n), kwin in zip(js, rn, kwins):
                dq = pl.ds(pl.multiple_of(j * (2 * DIL_BLOCK), 2 * DIL_BLOCK), 2 * DIL_BLOCK)
                variant = jnp.where(n == 0, 1, 0) + jnp.where(n == nb - 1, 2, 0)
                s = lax.dot_general(qs[dq, :], ks[kwin, :], _NT, preferred_element_type=F32)
                scores.append(s + bias_ref[variant])
            stats = []
            for s in scores:
                m = jnp.max(s, axis=-1, keepdims=True)
                stats.append((jnp.exp2((s - m).astype(BF16)), m))
            outs = [jnp.dot(p, jnp.concatenate([vs[kwin, :], ones_blk], axis=1),
                            preferred_element_type=F32)
                    for (p, _), kwin in zip(stats, kwins)]
            for (r, n), o2, (_, m) in zip(rn, outs, stats):
                o_b = jnp.where(is_h0, o2[:DIL_BLOCK, :LANES], o2[DIL_BLOCK:, :LANES])
                l_b = jnp.where(is_h0, o2[:DIL_BLOCK, LANES:], o2[DIL_BLOCK:, LANES:])
                m_b = jnp.where(is_h0, m[:DIL_BLOCK], m[DIL_BLOCK:])
                rows = pl.ds(r + n * (DIL_BLOCK * d), DIL_BLOCK, stride=d)
                if first:
                    acc[rows, :] = o_b
                    m_s[rows, :] = m_b
                    l_s[rows, :] = l_b
                else:
                    m_old = m_s[rows, :]
                    m_new = jnp.maximum(m_old, m_b)
                    a = jnp.exp2(m_old - m_new)
                    b = jnp.exp2(m_b - m_new)
                    acc[rows, :] = acc[rows, :] * a + o_b * b
                    l_s[rows, :] = l_s[rows, :] * a + l_b * b
                    m_s[rows, :] = m_new
            return carry

        lax.fori_loop(0, d * nb // DIL_UNROLL, blocks, 0)

    def finish(c, carry):
        rows = pl.ds(pl.multiple_of(c * (2 * DIL_BLOCK), 2 * DIL_BLOCK), 2 * DIL_BLOCK)
        o_ref[rows, :] = (acc[rows, :] / l_s[rows, :]).astype(o_ref.dtype)
        return carry

    lax.fori_loop(0, seq // (2 * DIL_BLOCK), finish, 0)


def _dil_bias():
    q = np.arange(DIL_BLOCK)[:, None]
    k = np.arange(2 * DIL_BLOCK)[None, :]
    band = np.abs(k - DIL_HALF - q) <= DIL_HALF
    first = k >= DIL_HALF
    last = k < DIL_HALF + DIL_BLOCK
    masks = np.stack([band, band & first, band & last, band & first & last])
    bias = np.where(masks, 0.0, NEG_BIG).astype(np.float32)
    return np.concatenate([bias, bias], axis=1)


def _dilated_attention(qd, kd, vd, seq):
    b = qd.shape[0]
    assert (seq // DIL_BLOCK) % DIL_UNROLL == 0
    max_rows = max(seq + 2 * DIL_HALF * d for _, d in DIL_BRANCHES)
    blk = pl.BlockSpec((None, seq, LANES), lambda i, h: (i, 0, h))
    return pl.pallas_call(
        functools.partial(_dil_kernel, seq=seq),
        grid=(b, D_DIL // LANES),
        in_specs=[blk, blk, blk,
                  pl.BlockSpec((4, 2 * DIL_BLOCK, 2 * DIL_BLOCK), lambda i, h: (0, 0, 0))],
        out_specs=blk,
        out_shape=jax.ShapeDtypeStruct((b, seq, D_DIL), BF16),
        scratch_shapes=[pltpu.VMEM((2 * seq, LANES), BF16),
                        pltpu.VMEM((max_rows, LANES), BF16), pltpu.VMEM((max_rows, LANES), BF16),
                        pltpu.VMEM((seq, LANES), F32), pltpu.VMEM((seq, LANES), F32),
                        pltpu.VMEM((seq, LANES), F32),
                        pltpu.VMEM((DIL_BLOCK * max(d for _, d in DIL_BRANCHES), LANES), F32)],
        compiler_params=pltpu.CompilerParams(dimension_semantics=("parallel", "parallel"),
                                             vmem_limit_bytes=VMEM_LIMIT),
        name="dilated_attention",
    )(qd, kd, vd, jnp.asarray(_dil_bias()))


def _na_classes(seq):
    nb = seq // NA_QBLOCK
    n_top = NA_WIN_BACK // NA_QBLOCK
    n_bot = nb - 1 - (seq - NA_KWIN + NA_WIN_BACK) // NA_QBLOCK
    return nb, n_top, n_bot


def _na_kernel(q_ref, k_ref, v_ref, bias_ref, hmask_ref, o_ref, *, seq):
    nb, n_top, n_bot = _na_classes(seq)
    lane = lax.broadcasted_iota(jnp.int32, (NA_QBLOCK, LANES), 1)
    is_h0 = lane < HEAD_DIM
    ones_blk = jnp.ones((NA_KWIN, LANES), BF16)

    def blocks(ii, carry):
        idx = [ii * NA_UNROLL + t for t in range(NA_UNROLL)]
        rows = [pl.ds(pl.multiple_of(i * NA_QBLOCK, NA_QBLOCK), NA_QBLOCK) for i in idx]
        wins = [pl.ds(pl.multiple_of(jnp.clip(i * NA_QBLOCK - NA_WIN_BACK, 0, seq - NA_KWIN),
                                     GRID_W), NA_KWIN) for i in idx]
        scores = []
        for i, row, win in zip(idx, rows, wins):
            variant = jnp.where(i < n_top, i,
                                jnp.where(i >= nb - n_bot, i - (nb - n_bot) + n_top + 1, n_top))
            qb = q_ref[row, :]
            q2 = jnp.concatenate([qb * hmask_ref[0:1, :], qb * hmask_ref[1:2, :]], axis=0)
            s = lax.dot_general(q2, k_ref[win, :], _NT, preferred_element_type=F32)
            scores.append(s + bias_ref[variant])
        probs = [jnp.exp2((s - jnp.max(s, axis=-1, keepdims=True)).astype(BF16)) for s in scores]
        for row, win, p in zip(rows, wins, probs):
            vw = jnp.concatenate([v_ref[win, :], ones_blk], axis=1)
            o2 = jnp.dot(p, vw, preferred_element_type=F32)
            o2 = o2[:, :LANES] / o2[:, LANES:]
            o_ref[row, :] = jnp.where(is_h0, o2[:NA_QBLOCK], o2[NA_QBLOCK:]).astype(o_ref.dtype)
        return carry

    lax.fori_loop(0, nb // NA_UNROLL, blocks, 0)


def _na_geometry(seq):
    rows = seq // GRID_W
    nb, n_top, n_bot = _na_classes(seq)
    assert rows >= 2 * NA_ROWS and nb % NA_UNROLL == 0 and nb > n_top + n_bot
    reps = np.array(list(range(n_top + 1)) + [nb - n_bot + t for t in range(n_bot)])
    ws = np.clip(reps * NA_QBLOCK - NA_WIN_BACK, 0, seq - NA_KWIN)
    qtok = reps[:, None] * NA_QBLOCK + np.arange(NA_QBLOCK)[None, :]
    ktok = ws[:, None] + np.arange(NA_KWIN)[None, :]
    qr, qc = qtok // GRID_W, qtok % GRID_W
    kr, kc = ktok // GRID_W, ktok % GRID_W
    row_start = np.clip(qr - NA_ROWS // 2, 0, rows - NA_ROWS)
    col_start = np.clip(qc - NA_COLS // 2, 0, GRID_W - NA_COLS)
    kr_, kc_ = kr[:, None, :], kc[:, None, :]
    valid = ((kr_ >= row_start[:, :, None]) & (kr_ < row_start[:, :, None] + NA_ROWS)
             & (kc_ >= col_start[:, :, None]) & (kc_ < col_start[:, :, None] + NA_COLS))
    ri = np.clip(kr_ - qr[:, :, None] + NA_ROWS - 1, 0, 2 * NA_ROWS - 2)
    return ri[:, ::GRID_W, ::GRID_W], valid


def _na_bias(rpb_l, seq):
    ri_tile, valid = _na_geometry(seq)
    n_h = rpb_l.shape[0]
    n_cls = ri_tile.shape[0]
    edge = GRID_W - NA_COLS
    padded = jnp.pad(rpb_l.astype(F32), ((0, 0), (0, 0), (edge, edge)), mode="edge")
    toep = jnp.stack([padded[:, :, GRID_W - 1 - qc:2 * GRID_W - 1 - qc] for qc in range(GRID_W)],
                     axis=2)
    classes = []
    for v in range(n_cls):
        qrows = [jnp.concatenate([toep[:, ri_tile[v, a, j]] for j in range(ri_tile.shape[2])],
                                 axis=-1) for a in range(ri_tile.shape[1])]
        classes.append(jnp.concatenate(qrows, axis=-2))
    bias = jnp.where(valid[None], jnp.stack(classes, axis=1) * LOG2E, NEG_BIG)
    bias = bias.reshape(n_h // 2, 2, n_cls, NA_QBLOCK, NA_KWIN).transpose(0, 2, 1, 3, 4)
    return bias.reshape(n_h // 2, n_cls, 2 * NA_QBLOCK, NA_KWIN)


def _neighbourhood_attention(qn, kn, vn, bias, seq):
    b = qn.shape[0]
    hmask = jnp.asarray((np.arange(LANES)[None, :] // HEAD_DIM == np.arange(2)[:, None]), BF16)
    blk = pl.BlockSpec((None, seq, LANES), lambda i, h: (i, 0, h))
    return pl.pallas_call(
        functools.partial(_na_kernel, seq=seq),
        grid=(b, D_NA // LANES),
        in_specs=[blk, blk, blk,
                  pl.BlockSpec((None,) + bias.shape[1:], lambda i, h: (h, 0, 0, 0)),
                  pl.BlockSpec((2, LANES), lambda i, h: (0, 0))],
        out_specs=blk,
        out_shape=jax.ShapeDtypeStruct((b, seq, D_NA), BF16),
        compiler_params=pltpu.CompilerParams(dimension_semantics=("parallel", "parallel"),
                                             vmem_limit_bytes=VMEM_LIMIT),
        name="neighbourhood_attention",
    )(qn, kn, vn, bias, hmask)


def _mix_ffn_kernel(x_ref, yd_ref, yn_ref, gb_ref, u_ref, up_ref, un_ref, cw_ref, wo_ref,
                    g_mix_ref, g_pre_ref, wg_ref, wu_ref, wd_ref, g_post_ref, o_ref, ubuf,
                    *, tiles_per_seq):
    tm = x_ref.shape[0]
    pos = pl.program_id(0) % tiles_per_seq
    keep_prev = jnp.where(pos == 0, 0.0, 1.0)
    keep_next = jnp.where(pos == tiles_per_seq - 1, 0.0, 1.0)
    ubuf[0:8, :] = up_ref[...] * keep_prev
    ubuf[8:8 + tm, :] = u_ref[...]
    ubuf[8 + tm:16 + tm, :] = un_ref[...] * keep_next
    hs = tm // ROW_SPLIT
    groups = [slice(h * hs, (h + 1) * hs) for h in range(ROW_SPLIT)]

    def dots(lhs, w):
        return [jnp.dot(a, w, preferred_element_type=F32) for a in lhs]

    cats = []
    for sl in groups:
        lo = 8 + sl.start
        conv = (cw_ref[0:1, :] * ubuf[lo - 1:lo - 1 + hs, :] + cw_ref[1:2, :] * ubuf[lo:lo + hs, :]
                + cw_ref[2:3, :] * ubuf[lo + 1:lo + 1 + hs, :])
        y_conv = (gb_ref[sl, :] * conv).astype(BF16)
        cats.append(jnp.concatenate([yd_ref[sl, :], yn_ref[sl, :], y_conv], axis=-1))
    mixed = dots(cats, wo_ref[...])
    x1 = [x_ref[sl, :] + _rms(m, g_mix_ref[...]) for sl, m in zip(groups, mixed)]
    hf = [_rms(v, g_pre_ref[...]).astype(BF16) for v in x1]
    f = [None] * ROW_SPLIT
    for lo, hi in FF_CHUNKS:
        gate = dots(hf, wg_ref[:, lo:hi])
        up = dots(hf, wu_ref[:, lo:hi])
        act = [(g * (1.0 / (1.0 + jnp.exp(-g))) * v).astype(BF16) for g, v in zip(gate, up)]
        part = dots(act, wd_ref[lo:hi, :])
        f = [p if a is None else a + p for a, p in zip(f, part)]
    for sl, v, a in zip(groups, x1, f):
        o_ref[sl, :] = v + _rms(a, g_post_ref[...])


def _mix_ffn(x2, yd, yn, gb, u, p, seq):
    t = x2.shape[0]
    tm = TOKEN_TILE
    sub = tm // 8
    last = t // 8 - 1
    row = lambda w: pl.BlockSpec((tm, w), lambda i: (i, 0))
    vec = pl.BlockSpec((1, D_MODEL), lambda i: (0, 0))
    resident = lambda shape: pl.BlockSpec(shape, lambda i: (0, 0), pipeline_mode=pl.Buffered(1))
    return pl.pallas_call(
        functools.partial(_mix_ffn_kernel, tiles_per_seq=seq // tm),
        grid=(t // tm,),
        in_specs=[row(D_MODEL), row(D_DIL), row(D_NA), row(D_CONV), row(D_CONV),
                  pl.BlockSpec((8, D_CONV), lambda i: (jnp.maximum(i * sub - 1, 0), 0)),
                  pl.BlockSpec((8, D_CONV), lambda i: (jnp.minimum((i + 1) * sub, last), 0)),
                  pl.BlockSpec((3, D_CONV), lambda i: (0, 0)),
                  resident((D_MODEL, D_MODEL)), vec, vec,
                  resident((D_MODEL, D_FF)), resident((D_MODEL, D_FF)), resident((D_FF, D_MODEL)),
                  vec],
        out_specs=row(D_MODEL),
        out_shape=jax.ShapeDtypeStruct((t, D_MODEL), F32),
        scratch_shapes=[pltpu.VMEM((tm + 16, D_CONV), F32)],
        compiler_params=pltpu.CompilerParams(dimension_semantics=("parallel",),
                                             vmem_limit_bytes=VMEM_LIMIT),
        name="mix_ffn",
    )(x2, yd, yn, gb, u, u, u, p["conv_w"], p["w_out"], p["g_post_mix"], p["g_pre_ffn"],
      p["w_gate"], p["w_up"], p["w_down"], p["g_post_ffn"])


def _rope_tables(seq):
    pos = jnp.arange(seq, dtype=F32)
    inv = 1.0 / (ROPE_THETA ** (jnp.arange(0, HEAD_DIM, 2, dtype=F32) / HEAD_DIM))
    ang = pos[:, None] * inv[None, :]
    reps = LANES // (HEAD_DIM // 2)
    cos_t = jnp.tile(jnp.cos(ang), (1, reps))
    sin_t = jnp.tile(jnp.sin(ang), (1, reps))
    first_half = (np.arange(LANES) % HEAD_DIM) < HEAD_DIM // 2
    sina_t = jnp.where(first_half[None, :], -sin_t, 0.0)
    sinb_t = jnp.where(first_half[None, :], 0.0, sin_t)
    return cos_t, sina_t, sinb_t


def _layer(x, tables, p):
    b, seq, _ = x.shape
    assert seq % (DIL_BLOCK * max(d for _, d in DIL_BRANCHES)) == 0
    assert seq % TOKEN_TILE == 0 and seq % INPROJ_TILE == 0
    x2 = x.reshape(b * seq, D_MODEL)
    qd, kd, vd, qn, kn, vn, gb, u = _inproj(x2, p["g_pre_mix"], p["w_in"], *tables, seq)
    to3 = lambda a: a.reshape(b, seq, a.shape[-1])
    yd = _dilated_attention(to3(qd), to3(kd), to3(vd), seq).reshape(b * seq, D_DIL)
    na_bias = _na_bias(p["rpb"], seq)
    yn = _neighbourhood_attention(to3(qn), to3(kn), to3(vn), na_bias, seq).reshape(b * seq, D_NA)
    out = _mix_ffn(x2, yd, yn, gb, u, p, seq)
    return out.reshape(b, seq, D_MODEL)


def kernel(x_prompt, x_sample, g_pre_mix, w_in, conv_w, rpb, w_out, g_post_mix, g_pre_ffn,
           w_gate, w_up, w_down, g_post_ffn):
    depth = w_in.shape[0]
    streams = [x_prompt, x_sample]
    tables = [_rope_tables(x.shape[1]) for x in streams]
    for l in range(depth):
        p = dict(
            g_pre_mix=g_pre_mix[l][None, :], w_in=w_in[l].astype(BF16), conv_w=conv_w[l],
            rpb=rpb[l], w_out=w_out[l].astype(BF16), g_post_mix=g_post_mix[l][None, :],
            g_pre_ffn=g_pre_ffn[l][None, :], w_gate=w_gate[l].astype(BF16),
            w_up=w_up[l].astype(BF16), w_down=w_down[l].astype(BF16),
            g_post_ffn=g_post_ffn[l][None, :])
        streams = [_layer(x, t, p) for x, t in zip(streams, tables)]
    return tuple(streams)
```
